```python
import jax, jax.numpy as jnp
from jax import lax
import numpy as np

D_MODEL = 2048
BATCH = 2
SEQ = 8192
DEPTH = 2
DEC_BATCH = 32
DEC_SEQ = 64
PAST_LEN = 1024

CHUNK = 64
CHUNK_MLP = 128
D_A = 1024
G_A = 8
DG_A = D_A // G_A
D_B = 1024
HEAD_B = 64
H_B = D_B // HEAD_B
LORA_W = 64
LORA_A = 64
LORA_G = 160
D_SHIFT = 3 * D_B + LORA_W + LORA_A + LORA_G
D_IN = 2 * D_A + D_SHIFT
SPLITS_B = (D_B, 2 * D_B, 3 * D_B, 3 * D_B + LORA_W, 3 * D_B + LORA_W + LORA_A)
D_FF = 4 * D_MODEL
RMS_EPS = 1e-5
LN_EPS = 1e-5
GN_EPS = 64e-5

kernel_name = 'hybrid_gmlp_rwkv7_stream_step'


def rmsnorm(x, g):
    xf = x.astype(jnp.float32)
    y = xf * lax.rsqrt(jnp.mean(xf * xf, axis=-1, keepdims=True) + RMS_EPS)
    return (y * g.astype(jnp.float32)).astype(x.dtype)


def layernorm(x, g, b, eps):
    xf = x.astype(jnp.float32)
    mu = jnp.mean(xf, axis=-1, keepdims=True)
    var = jnp.mean(jnp.square(xf - mu), axis=-1, keepdims=True)
    y = (xf - mu) * lax.rsqrt(var + eps) * g.astype(jnp.float32) + b.astype(jnp.float32)
    return y.astype(x.dtype)


def spatial_gating(u, v, ln_g, ln_b, w_s, b_s):
    bsz, t, _ = v.shape
    vn = layernorm(v, ln_g, ln_b, LN_EPS)
    n_blk = -(-t // CHUNK_MLP)
    vp = jnp.pad(vn, ((0, 0), (0, n_blk * CHUNK_MLP - t), (0, 0)))
    vp = vp.reshape(bsz, n_blk, CHUNK_MLP, G_A, DG_A)
    causal = jnp.tril(jnp.ones((CHUNK_MLP, CHUNK_MLP), dtype=bool))
    w = jnp.where(causal[None], w_s, jnp.zeros_like(w_s)).astype(vp.dtype)
    s = jnp.einsum('gqk,bnkgd->bnqgd', w, vp) + b_s.T.astype(vp.dtype)[None, None, :, :, None]
    s = s.reshape(bsz, n_blk * CHUNK_MLP, D_A)[:, :t]
    return u * s, vn


def wkv7_step(S, inp):
    r_t, w_t, k_t, v_t, kk_t, a_t = inp
    sa = jnp.einsum('bhij,bhj->bhi', S, kk_t)
    S = (S * w_t[:, :, None, :]
         - sa[..., None] * (kk_t * a_t)[:, :, None, :]
         + v_t[..., None] * k_t[:, :, None, :])
    o = jnp.einsum('bhij,bhj->bhi', S, r_t)
    return S, o


def wkv7_scan(s0, r, w, k, v, kk, a):
    xs = tuple(jnp.moveaxis(z, 1, 0) for z in (r, w, k, v, kk, a))
    s_fin, o = lax.scan(wkv7_step, s0, xs)
    return s_fin, jnp.moveaxis(o, 0, 1)


def rwkv7_branch(p, shift_prev, s0, mu, w0, w2, a0, a2, g2, k_k, k_a, r_k, gn_g, gn_b):
    f32 = jnp.float32
    bsz, t, _ = p.shape
    prev = jnp.concatenate([shift_prev[:, None, :].astype(p.dtype), p[:, :-1]], axis=1)
    xs = p + (prev - p) * mu
    r, k, v, xw, xa, xg = jnp.split(xs, SPLITS_B, axis=-1)
    logw = -jax.nn.softplus(-(w0 + jnp.tanh(xw) @ w2).astype(f32)) - 0.5
    decay = jnp.exp(-jnp.exp(logw))
    a = jax.nn.sigmoid((a0 + xa @ a2).astype(f32))
    g = jax.nn.sigmoid(xg) @ g2

    def heads(z):
        return z.astype(f32).reshape(bsz, t, H_B, HEAD_B)

    r_h, k_h, v_h, a_h, w_h = heads(r), heads(k), heads(v), heads(a), heads(decay)
    kk = k_h * k_k.astype(f32).reshape(H_B, HEAD_B)
    kk = kk * lax.rsqrt(jnp.maximum(jnp.sum(kk * kk, axis=-1, keepdims=True), 1e-24))
    k_h = k_h * (1 + (a_h - 1) * k_a.astype(f32).reshape(H_B, HEAD_B))
    s_new, o = wkv7_scan(s0.astype(f32), r_h, w_h, k_h, v_h, kk, a_h)
    mean = jnp.mean(o, axis=-1, keepdims=True)
    var = jnp.mean(jnp.square(o - mean), axis=-1, keepdims=True)
    on = ((o - mean) * lax.rsqrt(var + GN_EPS)).reshape(bsz, t, D_B)
    on = on * gn_g.astype(f32) + gn_b.astype(f32)
    bonus = jnp.sum(r_h * k_h * r_k.astype(f32), axis=-1, keepdims=True) * v_h
    y = ((on + bonus.reshape(bsz, t, D_B)) * g.astype(f32)).astype(p.dtype)
    return y, p[:, -1], s_new.astype(s0.dtype)


def trunk_layer(x, shift_prev, s0, prm):
    h = rmsnorm(x, prm['norm1'])
    proj = h @ prm['w_in']
    u = jax.nn.gelu(proj[..., :D_A])
    v = jax.nn.gelu(proj[..., D_A:2 * D_A])
    y_a, v_rows = spatial_gating(u, v, prm['ln_v_g'], prm['ln_v_b'], prm['w_s'], prm['b_s'])
    y_b, new_shift, s_new = rwkv7_branch(
        proj[..., 2 * D_A:], shift_prev, s0, prm['mu_shift'], prm['w0'], prm['w2'],
        prm['a0'], prm['a2'], prm['g2'], prm['k_k'], prm['k_a'], prm['r_k'],
        prm['gn_g'], prm['gn_b'])
    gates = jax.nn.sigmoid(h @ prm['w_gate'])
    mixed = (gates[..., :D_MODEL] * (y_a @ prm['w_pa'])
             + gates[..., D_MODEL:] * (y_b @ prm['w_pb']))
    x = x + mixed @ prm['w_o']
    h2 = rmsnorm(x, prm['norm2'])
    x = x + jnp.square(jax.nn.relu(h2 @ prm['w_up'])) @ prm['w_down']
    return x, v_rows, new_shift, s_new


def setup_inputs(seed: int = 0) -> dict:
    key = jax.random.key(seed)
    ks = iter(jax.random.split(key, 40))
    f32 = jnp.float32

    def nrm(shape, scale):
        return jax.random.normal(next(ks), shape, f32) * scale

    def unif(shape, lo, hi):
        return jax.random.uniform(next(ks), shape, f32, lo, hi)

    L = DEPTH
    return {
        'x_prompt': nrm((BATCH, SEQ, D_MODEL), 1.0),
        'x_sample': nrm((DEC_BATCH, DEC_SEQ, D_MODEL), 1.0),
        'state_tshift': nrm((L, DEC_BATCH, D_SHIFT), 1.0),
        'state_wkv': nrm((L, DEC_BATCH, H_B, HEAD_B, HEAD_B), 0.3),
        'norm1': 1.0 + nrm((L, D_MODEL), 0.02),
        'w_in': nrm((L, D_MODEL, D_IN), D_MODEL ** -0.5),
        'ln_v_g': 1.0 + nrm((L, D_A), 0.02),
        'ln_v_b': nrm((L, D_A), 0.02),
        'w_s': nrm((L, G_A, CHUNK_MLP, CHUNK_MLP), CHUNK_MLP ** -0.5),
        'b_s': 1.0 + nrm((L, G_A, CHUNK_MLP), 0.02),
        'mu_shift': unif((L, D_SHIFT), 0.0, 1.0),
        'w0': unif((L, D_B), -4.0, 0.0),
        'w2': nrm((L, LORA_W, D_B), 0.1 * LORA_W ** -0.5),
        'a0': nrm((L, D_B), 0.1),
        'a2': nrm((L, LORA_A, D_B), LORA_A ** -0.5),
        'g2': nrm((L, LORA_G, D_B), LORA_G ** -0.5),
        'k_k': 0.85 + nrm((L, D_B), 0.02),
        'k_a': 1.0 + nrm((L, D_B), 0.02),
        'r_k': nrm((L, H_B, HEAD_B), 0.1),
        'gn_g': 1.0 + nrm((L, D_B), 0.02),
        'gn_b': nrm((L, D_B), 0.02),
        'w_gate': nrm((L, D_MODEL, 2 * D_MODEL), D_MODEL ** -0.5),
        'w_pa': nrm((L, D_A, D_MODEL), D_A ** -0.5),
        'w_pb': nrm((L, D_B, D_MODEL), D_B ** -0.5),
        'w_o': nrm((L, D_MODEL, D_MODEL), D_MODEL ** -0.5),
        'norm2': 1.0 + nrm((L, D_MODEL), 0.02),
        'w_up': nrm((L, D_MODEL, D_FF), D_MODEL ** -0.5),
        'w_down': nrm((L, D_FF, D_MODEL), D_FF ** -0.5),
        'norm_f': 1.0 + nrm((D_MODEL,), 0.02),
    }


def reference(x_prompt, x_sample, state_tshift, state_wkv, norm1, w_in, ln_v_g, ln_v_b,
              w_s, b_s, mu_shift, w0, w2, a0, a2, g2, k_k, k_a, r_k, gn_g, gn_b,
              w_gate, w_pa, w_pb, w_o, norm2, w_up, w_down, norm_f):
    xp, xs = x_prompt, x_sample
    bp = x_prompt.shape[0]
    zero_shift = jnp.zeros((bp, D_SHIFT), x_prompt.dtype)
    zero_wkv = jnp.zeros((bp, H_B, HEAD_B, HEAD_B), state_wkv.dtype)
    tsh_p, wkv_p, tsh_s, wkv_s, vrow_s = [], [], [], [], []
    for l in range(DEPTH):
        prm = {
            'norm1': norm1[l], 'w_in': w_in[l], 'ln_v_g': ln_v_g[l], 'ln_v_b': ln_v_b[l],
            'w_s': w_s[l], 'b_s': b_s[l], 'mu_shift': mu_shift[l], 'w0': w0[l], 'w2': w2[l],
            'a0': a0[l], 'a2': a2[l], 'g2': g2[l], 'k_k': k_k[l], 'k_a': k_a[l], 'r_k': r_k[l],
            'gn_g': gn_g[l], 'gn_b': gn_b[l], 'w_gate': w_gate[l], 'w_pa': w_pa[l],
            'w_pb': w_pb[l], 'w_o': w_o[l], 'norm2': norm2[l], 'w_up': w_up[l],
            'w_down': w_down[l],
        }
        xp, _, shp, wkp = trunk_layer(xp, zero_shift, zero_wkv, prm)
        xs, vrows, shs, wks = trunk_layer(xs, state_tshift[l], state_wkv[l], prm)
        tsh_p.append(shp)
        wkv_p.append(wkp)
        tsh_s.append(shs)
        wkv_s.append(wks)
        vrow_s.append(vrows)
    y_prompt = rmsnorm(xp, norm_f)
    y_sample = rmsnorm(xs, norm_f)
    return (y_prompt, y_sample, jnp.stack(tsh_p), jnp.stack(wkv_p), jnp.stack(tsh_s),
            jnp.stack(wkv_s), jnp.stack(vrow_s))
```

```python
import functools

import jax
import jax.numpy as jnp
from jax import lax
from jax.experimental import pallas as pl
from jax.experimental.pallas import tpu as pltpu

F32 = jnp.float32
BF16 = jnp.bfloat16

HEAD = 64
PAIR = 2 * HEAD
CHUNK = 64
MLP_BLOCK = 128
RMS_EPS = 1e-5
LN_EPS = 1e-5
GN_EPS = 64e-5
DECAY_SCALE = 0.6065306597126334
VMEM_LIMIT = 56 * 1024 * 1024

NN = (((1,), (0,)), ((), ()))
NT = (((1,), (1,)), ((), ()))


def _params(sem):
    return pltpu.CompilerParams(dimension_semantics=sem, vmem_limit_bytes=VMEM_LIMIT)


def _dot(a, b, dims=NN):
    return lax.dot_general(a, b, dims, preferred_element_type=F32)


def _split(x):
    hi = x.astype(BF16)
    lo = (x - hi.astype(F32)).astype(BF16)
    return hi, lo


def _mm3(a, b, dims=NN):
    ah, al = _split(a)
    bh, bl = _split(b)
    return _dot(ah, bh, dims) + (_dot(ah, bl, dims) + _dot(al, bh, dims))


def _mm2(a, b_exact):
    ah, al = _split(a)
    return _dot(ah, b_exact) + _dot(al, b_exact)


def _rms(x, g):
    return x * lax.rsqrt(jnp.mean(x * x, axis=-1, keepdims=True) + RMS_EPS) * g


def _gelu(x):
    return 0.5 * x * (1.0 + jnp.tanh(0.7978845608028654 * (x + 0.044715 * (x * x * x))))


def _sigmoid(x):
    return 1.0 / (1.0 + jnp.exp(-x))


def _identity(x):
    return x


def _row_tile(rows, cap):
    t = min(rows, cap)
    while rows % t:
        t //= 2
    return t


def _norm_matmul_kernel(x_ref, g_ref, w_ref, o_ref, h_ref, *, act):
    @pl.when(pl.program_id(1) == 0)
    def _():
        h_ref[...] = _rms(x_ref[...], g_ref[...]).astype(BF16)

    o_ref[...] = act(_dot(h_ref[...], w_ref[...])).astype(o_ref.dtype)


def norm_matmul(x, gain, w, act, tn, out_dtype=F32):
    rows, d = x.shape
    n = w.shape[1]
    tm = _row_tile(rows, 512)
    return pl.pallas_call(
        functools.partial(_norm_matmul_kernel, act=act),
        grid=(rows // tm, n // tn),
        in_specs=[
            pl.BlockSpec((tm, d), lambda i, j: (i, 0)),
            pl.BlockSpec((1, d), lambda i, j: (0, 0)),
            pl.BlockSpec((d, tn), lambda i, j: (0, j)),
        ],
        out_specs=pl.BlockSpec((tm, tn), lambda i, j: (i, j)),
        out_shape=jax.ShapeDtypeStruct((rows, n), out_dtype),
        scratch_shapes=[pltpu.VMEM((tm, d), BF16)],
        compiler_params=_params(("parallel", "arbitrary")),
        name="norm_matmul",
    )(x, gain, w)


def _gmlp_kernel(u_ref, v_ref, lg_ref, lb_ref, ws_ref, bs_ref, ya_ref, vn_ref, *, blk, groups):
    v = v_ref[...]
    mu = jnp.mean(v, axis=-1, keepdims=True)
    vc = v - mu
    var = jnp.mean(vc * vc, axis=-1, keepdims=True)
    vn = vc * lax.rsqrt(var + LN_EPS) * lg_ref[...] + lb_ref[...]
    vn_ref[...] = vn
    vnb = vn.astype(BF16)
    q_idx = lax.broadcasted_iota(jnp.int32, (blk, blk), 0)
    k_idx = lax.broadcasted_iota(jnp.int32, (blk, blk), 1)
    causal = k_idx <= q_idx
    dg = v.shape[1] // groups
    for g in range(groups):
        wg = jnp.where(causal, ws_ref[g], 0.0).astype(BF16)
        bg = bs_ref[g]
        for c in range(v.shape[0] // blk):
            rs = slice(c * blk, (c + 1) * blk)
            cs = slice(g * dg, (g + 1) * dg)
            s = _dot(wg, vnb[rs, cs]) + bg
            ya_ref[rs, cs] = (u_ref[rs, cs] * s).astype(ya_ref.dtype)


def gmlp(uv, ln_g, ln_b, ws, bs_full, blk):
    rows = uv.shape[0]
    d_a = uv.shape[1] // 2
    groups = ws.shape[0]
    tm = _row_tile(rows, max(blk, 256))
    return pl.pallas_call(
        functools.partial(_gmlp_kernel, blk=blk, groups=groups),
        grid=(rows // tm,),
        in_specs=[
            pl.BlockSpec((tm, d_a), lambda i: (i, 0)),
            pl.BlockSpec((tm, d_a), lambda i: (i, 1)),
            pl.BlockSpec((1, d_a), lambda i: (0, 0)),
            pl.BlockSpec((1, d_a), lambda i: (0, 0)),
            pl.BlockSpec((groups, blk, blk), lambda i: (0, 0, 0)),
            pl.BlockSpec((groups, blk, d_a // groups), lambda i: (0, 0, 0)),
        ],
        out_specs=[
            pl.BlockSpec((tm, d_a), lambda i: (i, 0)),
            pl.BlockSpec((tm, d_a), lambda i: (i, 0)),
        ],
        out_shape=[
            jax.ShapeDtypeStruct((rows, d_a), BF16),
            jax.ShapeDtypeStruct((rows, d_a), F32),
        ],
        compiler_params=_params(("parallel",)),
        name="gmlp",
    )(uv, uv, ln_g, ln_b, ws, bs_full)


def _prep_kernel(p_ref, halo_ref, s0_ref, mu_ref, w0_ref, a0_ref, kk_ref, ka_ref, wwa_ref, g2_ref,
                 e1_ref, e1t_ref, r_o, lw_o, k_o, v_o, p_o, q_o, g_o, *, blocks_per_stream, d_b):
    i = pl.program_id(0)
    pb = p_ref[...]
    first = jnp.where(i % blocks_per_stream == 0, s0_ref[...], halo_ref[7:8, :])
    row = lax.broadcasted_iota(jnp.int32, pb.shape, 0)
    prev = jnp.where(row == 0, first, pltpu.roll(pb, 1, 0))
    xs = pb + (prev - pb) * mu_ref[...]

    r = xs[:, 0:d_b]
    k = xs[:, d_b:2 * d_b]
    v = xs[:, 2 * d_b:3 * d_b]
    wa = xs[:, 3 * d_b:3 * d_b + PAIR]
    gl = xs[:, 3 * d_b + PAIR:]
    lane = lax.broadcasted_iota(jnp.int32, wa.shape, 1)
    lora = _mm3(jnp.where(lane < HEAD, jnp.tanh(wa), wa), wwa_ref[...])
    lw = -DECAY_SCALE * _sigmoid(w0_ref[...] + lora[:, :d_b])
    a = _sigmoid(a0_ref[...] + lora[:, d_b:])
    g = _mm3(_sigmoid(gl), g2_ref[...])

    kk = k * kk_ref[...]
    ss = _mm2(_mm2(kk * kk, e1_ref[...]), e1t_ref[...])
    kk = kk * lax.rsqrt(jnp.maximum(ss, 1e-24))

    r_o[...] = r
    lw_o[...] = lw
    k_o[...] = k * (1.0 + (a - 1.0) * ka_ref[...])
    v_o[...] = v
    p_o[...] = kk
    q_o[...] = -(kk * a)
    g_o[...] = g


def rwkv_prep(p, shift0, mu, w0, a0, k_k, k_a, wwa, g2p, e1, e1t, t_len):
    rows, width = p.shape
    d_b = w0.shape[1]
    tm = _row_tile(t_len, 256)
    bps = t_len // tm
    vec = lambda n: pl.BlockSpec((1, n), lambda i: (0, 0))
    full = lambda a: pl.BlockSpec(a.shape, lambda i: (0, 0))
    out = pl.BlockSpec((tm, d_b), lambda i: (i, 0))
    return pl.pallas_call(
        functools.partial(_prep_kernel, blocks_per_stream=bps, d_b=d_b),
        grid=(rows // tm,),
        in_specs=[
            pl.BlockSpec((tm, width), lambda i: (i, 0)),
            pl.BlockSpec((8, width), lambda i: (jnp.maximum(i * (tm // 8) - 1, 0), 0)),
            pl.BlockSpec((None, 1, width), lambda i: (i // bps, 0, 0)),
            vec(width), vec(d_b), vec(d_b), vec(d_b), vec(d_b),
            full(wwa), full(g2p), full(e1), full(e1t),
        ],
        out_specs=[out] * 7,
        out_shape=[jax.ShapeDtypeStruct((rows, d_b), F32)] * 7,
        compiler_params=_params(("parallel",)),
        name="rwkv_prep",
    )(p, p, shift0, mu, w0, a0, k_k, k_a, wwa, g2p, e1, e1t)


def _wkv_pair_chunk(r, lw, cum, k, v, p, q, g, rk, gng, gnb, S, consts):
    m0, bd_strict, sel_incl, bd_mask, eye, ones_bd = consts
    cum_end = cum[CHUNK - 1:CHUNK, :]
    g_in = jnp.exp(cum)
    g_ex = jnp.exp(cum - lw)
    g_inv = jnp.exp(-cum)
    g_bar = jnp.exp(cum_end - cum)
    pt, rt = p * g_ex, r * g_in
    qt, kt = q * g_inv, k * g_inv
    qb, kb = q * g_bar, k * g_bar

    zero = jnp.zeros_like(pt)
    stack = lambda x: jnp.concatenate([jnp.where(m0, x, zero), jnp.where(m0, zero, x)], axis=0)
    lhs = jnp.concatenate([stack(pt), stack(rt)], axis=0)
    rhs = jnp.concatenate([qt, qt, kt, kt], axis=0)
    a_all = _mm3(lhs, rhs, NT)
    c2 = 2 * CHUNK
    a_pq = jnp.where(bd_strict, a_all[:c2, :c2], 0.0)
    a_pk = jnp.where(bd_strict, a_all[:c2, c2:], 0.0)
    pick = lambda blk: jnp.where(m0, blk[:CHUNK], blk[CHUNK:])
    a_rq = jnp.where(sel_incl, pick(a_all[c2:, :c2]), 0.0)
    a_rk = jnp.where(sel_incl, pick(a_all[c2:, c2:]), 0.0)

    t_acc = eye + a_pq
    pw = _mm3(a_pq, a_pq)
    span = 2
    while span * 2 < CHUNK:
        both = _mm3(pw, jnp.concatenate([pw, t_acc], axis=1))
        pw, t_acc = both[:, :c2], t_acc + both[:, c2:]
        span *= 2
    t_acc = t_acc + _mm3(pw, t_acc)

    x_all = _mm3(jnp.concatenate([pt, rt], axis=0), S, NT)
    x_p, x_r = x_all[:CHUNK], x_all[CHUNK:]
    v_st = stack(v)
    u_st = _mm3(t_acc, stack(x_p) + _mm3(a_pk, v_st))
    o = x_r + _mm3(jnp.concatenate([a_rq, a_rk], axis=1), jnp.concatenate([u_st, v_st], axis=0))
    u = u_st[:CHUNK] + u_st[CHUNK:]
    upd = _mm3(jnp.concatenate([u, v], axis=0).T, jnp.concatenate([qb, kb], axis=0))
    s_new = S * jnp.exp(cum_end) + jnp.where(bd_mask, upd, 0.0)

    inv_n = 1.0 / HEAD
    mean = _mm2(o, ones_bd) * inv_n
    oc = o - mean
    var = _mm2(oc * oc, ones_bd) * inv_n
    on = oc * lax.rsqrt(var + GN_EPS) * gng + gnb
    bonus = _mm2(r * k * rk, ones_bd) * v
    y = (on + bonus) * g
    return y, s_new


def _wkv_kernel(r_ref, lw_ref, k_ref, v_ref, p_ref, q_ref, g_ref, rk_ref, gng_ref, gnb_ref, s0_ref,
                y_ref, so_ref, s_scr, *, n_pairs, n_chunks):
    tb = pl.program_id(2)

    @pl.when(tb == 0)
    def _():
        s_scr[...] = s0_ref[...]

    lane = lax.broadcasted_iota(jnp.int32, (CHUNK, PAIR), 1)
    m0 = lane < HEAD
    c2 = 2 * CHUNK
    ri = lax.broadcasted_iota(jnp.int32, (c2, c2), 0)
    ci = lax.broadcasted_iota(jnp.int32, (c2, c2), 1)
    same = (ri // CHUNK) == (ci // CHUNK)
    bd_strict = same & ((ci % CHUNK) < (ri % CHUNK))
    bd_mask = (ri // HEAD) == (ci // HEAD)
    eye = jnp.where(ri == ci, 1.0, 0.0).astype(F32)
    ones_bd = jnp.where(bd_mask, 1.0, 0.0).astype(BF16)
    ti = lax.broadcasted_iota(jnp.int32, (CHUNK, c2), 0)
    si = lax.broadcasted_iota(jnp.int32, (CHUNK, c2), 1)
    sel_incl = (si % CHUNK) <= ti
    consts = (m0, bd_strict, sel_incl, bd_mask, eye, ones_bd)
    tr = lax.broadcasted_iota(jnp.int32, (CHUNK, CHUNK), 0)
    tc = lax.broadcasted_iota(jnp.int32, (CHUNK, CHUNK), 1)
    tri = jnp.where(tc <= tr, 1.0, 0.0).astype(BF16)

    def chunk_body(c, carry):
        rows = pl.ds(pl.multiple_of(c * CHUNK, CHUNK), CHUNK)
        lw_all = lw_ref[rows, :]
        lw_hi, lw_lo = _split(lw_all)
        cum_all = _dot(tri, lw_hi) + _dot(tri, lw_lo)
        for pr in range(n_pairs):
            ls = slice(pr * PAIR, (pr + 1) * PAIR)
            y, s_new = _wkv_pair_chunk(
                r_ref[rows, ls], lw_all[:, ls], cum_all[:, ls], k_ref[rows, ls], v_ref[rows, ls],
                p_ref[rows, ls], q_ref[rows, ls], g_ref[rows, ls],
                rk_ref[:, ls], gng_ref[:, ls], gnb_ref[:, ls], s_scr[pr], consts)
            s_scr[pr] = s_new
            y_ref[rows, ls] = y.astype(y_ref.dtype)
        return carry

    lax.fori_loop(0, n_chunks, chunk_body, 0)

    @pl.when(tb == pl.num_programs(2) - 1)
    def _():
        so_ref[...] = s_scr[...]


def wkv(r, lw, k, v, p, q, g, rk, gng, gnb, s0_bd, t_len):
    rows, d_b = r.shape
    n_streams = rows // t_len
    lanes = 2 * PAIR
    n_pairs = lanes // PAIR
    tb_len = _row_tile(t_len, 256)
    tbs = t_len // tb_len
    seq = pl.BlockSpec((tb_len, lanes), lambda s, l, t: (s * tbs + t, l))
    vec = pl.BlockSpec((1, lanes), lambda s, l, t: (0, l))
    st = pl.BlockSpec((None, n_pairs, PAIR, PAIR), lambda s, l, t: (s, l, 0, 0))
    return pl.pallas_call(
        functools.partial(_wkv_kernel, n_pairs=n_pairs, n_chunks=tb_len // CHUNK),
        grid=(n_streams, d_b // lanes, tbs),
        in_specs=[seq] * 7 + [vec] * 3 + [st],
        out_specs=[seq, st],
        out_shape=[
            jax.ShapeDtypeStruct((rows, d_b), BF16),
            jax.ShapeDtypeStruct(s0_bd.shape, F32),
        ],
        scratch_shapes=[pltpu.VMEM((n_pairs, PAIR, PAIR), F32)],
        compiler_params=_params(("parallel", "parallel", "arbitrary")),
        name="wkv",
    )(r, lw, k, v, p, q, g, rk, gng, gnb, s0_bd)


def _mix_kernel(ya_ref, yb_ref, wa_ref, wb_ref, ga_ref, gb_ref, o_ref):
    o_ref[...] = (ga_ref[...] * _dot(ya_ref[...], wa_ref[...])
                  + gb_ref[...] * _dot(yb_ref[...], wb_ref[...])).astype(o_ref.dtype)


def mix(ya, yb, w_pa, w_pb, gates, tn=1024):
    rows, d_a = ya.shape
    d_b = yb.shape[1]
    d = w_pa.shape[1]
    tm = _row_tile(rows, 512)
    nj = d // tn
    return pl.pallas_call(
        _mix_kernel,
        grid=(rows // tm, nj),
        in_specs=[
            pl.BlockSpec((tm, d_a), lambda i, j: (i, 0)),
            pl.BlockSpec((tm, d_b), lambda i, j: (i, 0)),
            pl.BlockSpec((d_a, tn), lambda i, j: (0, j)),
            pl.BlockSpec((d_b, tn), lambda i, j: (0, j)),
            pl.BlockSpec((tm, tn), lambda i, j: (i, j)),
            pl.BlockSpec((tm, tn), lambda i, j: (i, j + nj)),
        ],
        out_specs=pl.BlockSpec((tm, tn), lambda i, j: (i, j)),
        out_shape=jax.ShapeDtypeStruct((rows, d), BF16),
        compiler_params=_params(("parallel", "parallel")),
        name="mix",
    )(ya, yb, w_pa, w_pb, gates, gates)


def _matmul_res_kernel(x_ref, m_ref, w_ref, o_ref):
    o_ref[...] = x_ref[...] + _dot(m_ref[...], w_ref[...])


def matmul_res(x, m, w, tn=1024):
    rows, d = x.shape
    kdim = m.shape[1]
    tm = _row_tile(rows, 512)
    return pl.pallas_call(
        _matmul_res_kernel,
        grid=(rows // tm, d // tn),
        in_specs=[
            pl.BlockSpec((tm, tn), lambda i, j: (i, j)),
            pl.BlockSpec((tm, kdim), lambda i, j: (i, 0)),
            pl.BlockSpec((kdim, tn), lambda i, j: (0, j)),
        ],
        out_specs=pl.BlockSpec((tm, tn), lambda i, j: (i, j)),
        out_shape=jax.ShapeDtypeStruct((rows, d), F32),
        compiler_params=_params(("parallel", "parallel")),
        name="matmul_res",
    )(x, m, w)


def _ffn_kernel(x_ref, g_ref, wu_ref, wd_ref, gf_ref, o_ref, h_ref, acc_ref, *, final_norm):
    f = pl.program_id(1)

    @pl.when(f == 0)
    def _():
        h_ref[...] = _rms(x_ref[...], g_ref[...]).astype(BF16)
        acc_ref[...] = jnp.zeros_like(acc_ref)

    up = jnp.maximum(_dot(h_ref[...], wu_ref[...]), 0.0)
    acc_ref[...] += _dot((up * up).astype(BF16), wd_ref[...])

    @pl.when(f == pl.num_programs(1) - 1)
    def _():
        y = x_ref[...] + acc_ref[...]
        if final_norm:
            y = _rms(y, gf_ref[...])
        o_ref[...] = y


def ffn(x, gain, w_up, w_down, final_gain, final_norm, tf=1024):
    rows, d = x.shape
    d_ff = w_up.shape[1]
    tm = _row_tile(rows, 512)
    return pl.pallas_call(
        functools.partial(_ffn_kernel, final_norm=final_norm),
        grid=(rows // tm, d_ff // tf),
        in_specs=[
            pl.BlockSpec((tm, d), lambda i, f: (i, 0)),
            pl.BlockSpec((1, d), lambda i, f: (0, 0)),
            pl.BlockSpec((d, tf), lambda i, f: (0, f)),
            pl.BlockSpec((tf, d), lambda i, f: (f, 0)),
            pl.BlockSpec((1, d), lambda i, f: (0, 0)),
        ],
        out_specs=pl.BlockSpec((tm, d), lambda i, f: (i, 0)),
        out_shape=jax.ShapeDtypeStruct((rows, d), F32),
        scratch_shapes=[pltpu.VMEM((tm, d), BF16), pltpu.VMEM((tm, d), F32)],
        compiler_params=_params(("parallel", "arbitrary")),
        name="ffn",
    )(x, gain, w_up, w_down, final_gain)


def _to_block_diag(s):
    b, h, n, _ = s.shape
    sr = s.reshape(b, h // 2, 2, n, n)
    z = jnp.zeros_like(sr[:, :, 0])
    top = jnp.concatenate([sr[:, :, 0], z], axis=-1)
    bot = jnp.concatenate([z, sr[:, :, 1]], axis=-1)
    return jnp.concatenate([top, bot], axis=-2)


def _from_block_diag(sbd):
    b, hp, n2, _ = sbd.shape
    n = n2 // 2
    return jnp.stack([sbd[:, :, :n, :n], sbd[:, :, n:, n:]], axis=2).reshape(b, 2 * hp, n, n)


def _layer(x, shift0, s0, t_len, wt, final_gain, final_norm):
    rows = x.shape[0]
    n_streams = rows // t_len
    d_b = wt["w0"].shape[1]
    d_shift = shift0.shape[1]
    width = wt["w_shift"].shape[1]

    uv = norm_matmul(x, wt["norm1"], wt["w_uv"], _gelu, tn=1024)
    pj = norm_matmul(x, wt["norm1"], wt["w_shift"], _identity, tn=width // 3)
    gates = norm_matmul(x, wt["norm1"], wt["w_gate"], _sigmoid, tn=1024)

    blk = MLP_BLOCK if t_len % MLP_BLOCK == 0 else t_len
    ya, vn = gmlp(uv, wt["ln_g"], wt["ln_b"], wt["w_s"][:, :blk, :blk], wt["b_full"][:, :blk], blk)

    shift0p = jnp.pad(shift0, ((0, 0), (0, width - d_shift)))[:, None, :]
    r, lw, k, v, p, q, g = rwkv_prep(pj, shift0p, wt["mu"], wt["w0"], wt["a0"], wt["k_k"], wt["k_a"],
                                     wt["wwa"], wt["g2p"], wt["e1"], wt["e1t"], t_len)
    yb, s_bd = wkv(r, lw, k, v, p, q, g, wt["r_k"], wt["gn_g"], wt["gn_b"], _to_block_diag(s0), t_len)

    mixed = mix(ya, yb, wt["w_pa"], wt["w_pb"], gates)
    x = matmul_res(x, mixed, wt["w_o"])
    x = ffn(x, wt["norm2"], wt["w_up"], wt["w_down"], final_gain, final_norm)

    new_shift = pj.reshape(n_streams, t_len, width)[:, -1, :d_shift]
    return x, vn, new_shift, _from_block_diag(s_bd)


def kernel(x_prompt, x_sample, state_tshift, state_wkv, norm1, w_in, ln_v_g, ln_v_b, w_s, b_s, mu_shift,
           w0, w2, a0, a2, g2, k_k, k_a, r_k, gn_g, gn_b, w_gate, w_pa, w_pb, w_o, norm2, w_up, w_down,
           norm_f):
    depth = w_in.shape[0]
    bp, tp, d = x_prompt.shape
    bs, ts, _ = x_sample.shape
    d_a = ln_v_g.shape[1]
    d_b = w0.shape[1]
    d_shift = mu_shift.shape[1]
    heads = d_b // HEAD
    assert tp % CHUNK == 0 and ts % CHUNK == 0 and heads % 4 == 0
    assert all(t % MLP_BLOCK == 0 or (t < MLP_BLOCK and t % 8 == 0) for t in (tp, ts))
    lo_w, lo_a = w2.shape[1], a2.shape[1]
    assert lo_w == HEAD and lo_a == HEAD
    lo_g = g2.shape[1]
    width = -(-d_shift // 384) * 384
    g_w = width - (3 * d_b + PAIR)
    dg = d_a // w_s.shape[1]

    head_of = jnp.arange(d_b) // HEAD
    e1 = (head_of[:, None] == jnp.arange(PAIR)[None, :]).astype(BF16)
    row = lambda a: a.reshape(1, -1)

    xp = x_prompt.reshape(bp * tp, d)
    xs = x_sample.reshape(bs * ts, d)
    zero_shift = jnp.zeros((bp, d_shift), F32)
    zero_wkv = jnp.zeros((bp, heads, HEAD, HEAD), F32)
    outs = {n: [] for n in ("tsh_p", "wkv_p", "tsh_s", "wkv_s", "vrow_s")}
    for l in range(depth):
        wwa = jnp.zeros((PAIR, 2 * d_b), F32)
        wwa = wwa.at[:lo_w, :d_b].set(w2[l]).at[lo_w:, d_b:].set(a2[l])
        wt = {
            "norm1": row(norm1[l]), "norm2": row(norm2[l]),
            "w_uv": w_in[l][:, :2 * d_a].astype(BF16),
            "w_shift": jnp.pad(w_in[l][:, 2 * d_a:], ((0, 0), (0, width - d_shift))).astype(BF16),
            "w_gate": w_gate[l].astype(BF16),
            "ln_g": row(ln_v_g[l]), "ln_b": row(ln_v_b[l]),
            "w_s": w_s[l],
            "b_full": jnp.broadcast_to(b_s[l][:, :, None], b_s[l].shape + (dg,)),
            "mu": row(jnp.pad(mu_shift[l], (0, width - d_shift))),
            "w0": row(w0[l]), "a0": row(a0[l]), "k_k": row(k_k[l]), "k_a": row(k_a[l]),
            "wwa": wwa,
            "g2p": jnp.pad(g2[l], ((0, g_w - lo_g), (0, 0))),
            "e1": e1, "e1t": e1.T,
            "r_k": row(r_k[l]), "gn_g": row(gn_g[l]), "gn_b": row(gn_b[l]),
            "w_pa": w_pa[l].astype(BF16), "w_pb": w_pb[l].astype(BF16), "w_o": w_o[l].astype(BF16),
            "w_up": w_up[l].astype(BF16), "w_down": w_down[l].astype(BF16),
        }
        last = l == depth - 1
        gf = row(norm_f)
        xp, _, shp, wkp = _layer(xp, zero_shift, zero_wkv, tp, wt, gf, last)
        xs, vrows, shs, wks = _layer(xs, state_tshift[l], state_wkv[l], ts, wt, gf, last)
        outs["tsh_p"].append(shp)
        outs["wkv_p"].append(wkp)
        outs["tsh_s"].append(shs)
        outs["wkv_s"].append(wks)
        outs["vrow_s"].append(vrows.reshape(bs, ts, d_a))
    return (xp.reshape(bp, tp, d), xs.reshape(bs, ts, d), jnp.stack(outs["tsh_p"]), jnp.stack(outs["wkv_p"]),
            jnp.stack(outs["tsh_s"]), jnp.stack(outs["wkv_s"]), jnp.stack(outs["vrow_s"]))
```

```python
import functools

import jax
import jax.numpy as jnp
from jax import lax
from jax.experimental import pallas as pl
from jax.experimental.pallas import tpu as pltpu

F32 = jnp.float32
BF16 = jnp.bfloat16

HEAD = 64
PAIR = 2 * HEAD
CHUNK = 64
WKV_PAIRS = 8
MLP_BLOCK = 128
RMS_EPS = 1e-5
LN_EPS = 1e-5
GN_EPS = 64e-5
DECAY_SCALE = 0.6065306597126334
VMEM_LIMIT = 56 * 1024 * 1024

NN = (((1,), (0,)), ((), ()))
NT = (((1,), (1,)), ((), ()))


def _params(sem):
    return pltpu.CompilerParams(dimension_semantics=sem, vmem_limit_bytes=VMEM_LIMIT)


def _dot(a, b, dims=NN):
    return lax.dot_general(a, b, dims, preferred_element_type=F32)


def _split(x):
    hi = x.astype(BF16)
    lo = (x - hi.astype(F32)).astype(BF16)
    return hi, lo


def _mm3(a, b, dims=NN):
    ah, al = _split(a)
    bh, bl = _split(b)
    return _dot(ah, bh, dims) + (_dot(ah, bl, dims) + _dot(al, bh, dims))


def _mm2(a, b_exact):
    ah, al = _split(a)
    return _dot(ah, b_exact) + _dot(al, b_exact)


def _rms(x, g):
    return x * lax.rsqrt(jnp.mean(x * x, axis=-1, keepdims=True) + RMS_EPS) * g


def _gelu(x):
    return 0.5 * x * (1.0 + jnp.tanh(0.7978845608028654 * (x + 0.044715 * (x * x * x))))


def _sigmoid(x):
    return 1.0 / (1.0 + jnp.exp(-x))


def _identity(x):
    return x


def _row_tile(rows, cap):
    t = min(rows, cap)
    while rows % t:
        t //= 2
    return t


def _norm_matmul_kernel(x_ref, g_ref, w_ref, o_ref, h_ref, *, act):
    @pl.when(pl.program_id(1) == 0)
    def _():
        h_ref[...] = _rms(x_ref[...], g_ref[...]).astype(BF16)

    o_ref[...] = act(_dot(h_ref[...], w_ref[...])).astype(o_ref.dtype)


def norm_matmul(x, gain, w, act, tn, out_dtype=F32):
    rows, d = x.shape
    n = w.shape[1]
    tm = _row_tile(rows, 512)
    return pl.pallas_call(
        functools.partial(_norm_matmul_kernel, act=act),
        grid=(rows // tm, n // tn),
        in_specs=[
            pl.BlockSpec((tm, d), lambda i, j: (i, 0)),
            pl.BlockSpec((1, d), lambda i, j: (0, 0)),
            pl.BlockSpec((d, tn), lambda i, j: (0, j)),
        ],
        out_specs=pl.BlockSpec((tm, tn), lambda i, j: (i, j)),
        out_shape=jax.ShapeDtypeStruct((rows, n), out_dtype),
        scratch_shapes=[pltpu.VMEM((tm, d), BF16)],
        compiler_params=_params(("parallel", "arbitrary")),
        name="norm_matmul",
    )(x, gain, w)


def _gmlp_kernel(u_ref, v_ref, lg_ref, lb_ref, ws_ref, bs_ref, ya_ref, vn_ref, *, blk, groups):
    v = v_ref[...]
    mu = jnp.mean(v, axis=-1, keepdims=True)
    vc = v - mu
    var = jnp.mean(vc * vc, axis=-1, keepdims=True)
    vn = vc * lax.rsqrt(var + LN_EPS) * lg_ref[...] + lb_ref[...]
    vn_ref[...] = vn
    vnb = vn.astype(BF16)
    q_idx = lax.broadcasted_iota(jnp.int32, (blk, blk), 0)
    k_idx = lax.broadcasted_iota(jnp.int32, (blk, blk), 1)
    causal = k_idx <= q_idx
    dg = v.shape[1] // groups
    for g in range(groups):
        wg = jnp.where(causal, ws_ref[g], 0.0).astype(BF16)
        bg = bs_ref[g]
        for c in range(v.shape[0] // blk):
            rs = slice(c * blk, (c + 1) * blk)
            cs = slice(g * dg, (g + 1) * dg)
            s = _dot(wg, vnb[rs, cs]) + bg
            ya_ref[rs, cs] = (u_ref[rs, cs] * s).astype(ya_ref.dtype)


def gmlp(uv, ln_g, ln_b, ws, bs_full, blk):
    rows = uv.shape[0]
    d_a = uv.shape[1] // 2
    groups = ws.shape[0]
    tm = _row_tile(rows, max(blk, 256))
    return pl.pallas_call(
        functools.partial(_gmlp_kernel, blk=blk, groups=groups),
        grid=(rows // tm,),
        in_specs=[
            pl.BlockSpec((tm, d_a), lambda i: (i, 0)),
            pl.BlockSpec((tm, d_a), lambda i: (i, 1)),
            pl.BlockSpec((1, d_a), lambda i: (0, 0)),
            pl.BlockSpec((1, d_a), lambda i: (0, 0)),
            pl.BlockSpec((groups, blk, blk), lambda i: (0, 0, 0)),
            pl.BlockSpec((groups, blk, d_a // groups), lambda i: (0, 0, 0)),
        ],
        out_specs=[
            pl.BlockSpec((tm, d_a), lambda i: (i, 0)),
            pl.BlockSpec((tm, d_a), lambda i: (i, 0)),
        ],
        out_shape=[
            jax.ShapeDtypeStruct((rows, d_a), BF16),
            jax.ShapeDtypeStruct((rows, d_a), F32),
        ],
        compiler_params=_params(("parallel",)),
        name="gmlp",
    )(uv, uv, ln_g, ln_b, ws, bs_full)


def _prep_kernel(p_ref, halo_ref, s0_ref, mu_ref, w0_ref, a0_ref, kk_ref, ka_ref, wwa_ref, g2_ref,
                 e1_ref, e1t_ref, r_o, lw_o, k_o, v_o, p_o, q_o, g_o, *, blocks_per_stream, d_b):
    i = pl.program_id(0)
    pb = p_ref[...]
    first = jnp.where(i % blocks_per_stream == 0, s0_ref[...], halo_ref[7:8, :])
    row = lax.broadcasted_iota(jnp.int32, pb.shape, 0)
    prev = jnp.where(row == 0, first, pltpu.roll(pb, 1, 0))
    xs = pb + (prev - pb) * mu_ref[...]

    r = xs[:, 0:d_b]
    k = xs[:, d_b:2 * d_b]
    v = xs[:, 2 * d_b:3 * d_b]
    wa = xs[:, 3 * d_b:3 * d_b + PAIR]
    gl = xs[:, 3 * d_b + PAIR:]
    lane = lax.broadcasted_iota(jnp.int32, wa.shape, 1)
    lora = _mm3(jnp.where(lane < HEAD, jnp.tanh(wa), wa), wwa_ref[...])
    lw = -DECAY_SCALE * _sigmoid(w0_ref[...] + lora[:, :d_b])
    a = _sigmoid(a0_ref[...] + lora[:, d_b:])
    g = _mm3(_sigmoid(gl), g2_ref[...])

    kk = k * kk_ref[...]
    ss = _mm2(_mm2(kk * kk, e1_ref[...]), e1t_ref[...])
    kk = kk * lax.rsqrt(jnp.maximum(ss, 1e-24))

    r_o[...] = r
    lw_o[...] = lw
    k_o[...] = k * (1.0 + (a - 1.0) * ka_ref[...])
    v_o[...] = v
    p_o[...] = kk
    q_o[...] = -(kk * a)
    g_o[...] = g


def rwkv_prep(p, shift0, mu, w0, a0, k_k, k_a, wwa, g2p, e1, e1t, t_len):
    rows, width = p.shape
    d_b = w0.shape[1]
    tm = _row_tile(t_len, 256)
    bps = t_len // tm
    vec = lambda n: pl.BlockSpec((1, n), lambda i: (0, 0))
    full = lambda a: pl.BlockSpec(a.shape, lambda i: (0, 0))
    out = pl.BlockSpec((tm, d_b), lambda i: (i, 0))
    return pl.pallas_call(
        functools.partial(_prep_kernel, blocks_per_stream=bps, d_b=d_b),
        grid=(rows // tm,),
        in_specs=[
            pl.BlockSpec((tm, width), lambda i: (i, 0)),
            pl.BlockSpec((8, width), lambda i: (jnp.maximum(i * (tm // 8) - 1, 0), 0)),
            pl.BlockSpec((None, 1, width), lambda i: (i // bps, 0, 0)),
            vec(width), vec(d_b), vec(d_b), vec(d_b), vec(d_b),
            full(wwa), full(g2p), full(e1), full(e1t),
        ],
        out_specs=[out] * 7,
        out_shape=[jax.ShapeDtypeStruct((rows, d_b), F32)] * 7,
        compiler_params=_params(("parallel",)),
        name="rwkv_prep",
    )(p, p, shift0, mu, w0, a0, k_k, k_a, wwa, g2p, e1, e1t)


def _wkv_chunk(r, lw, cum, k, v, p, q, g, rk, gng, gnb, S, consts):
    m0, bd_strict, sel_incl, bd_mask, eye, ones_bd = consts
    c2 = 2 * CHUNK
    pm = lambda f, *ls: [f(*a) for a in zip(*ls)]
    zero = jnp.zeros((CHUNK, PAIR), F32)
    stack = lambda x: jnp.concatenate([jnp.where(m0, x, zero), jnp.where(m0, zero, x)], axis=0)
    pick = lambda blk: jnp.where(m0, blk[:CHUNK], blk[CHUNK:])

    cum_end = pm(lambda c: c[CHUNK - 1:CHUNK, :], cum)
    g_in = pm(jnp.exp, cum)
    g_ex = pm(lambda c, l: jnp.exp(c - l), cum, lw)
    g_inv = pm(lambda c: jnp.exp(-c), cum)
    g_bar = pm(lambda ce, c: jnp.exp(ce - c), cum_end, cum)
    mul = lambda a, b: a * b
    pt, rt = pm(mul, p, g_ex), pm(mul, r, g_in)
    qt, kt = pm(mul, q, g_inv), pm(mul, k, g_inv)
    qb, kb = pm(mul, q, g_bar), pm(mul, k, g_bar)

    a_all = pm(lambda a, b, c, d: _mm3(jnp.concatenate([stack(a), stack(b)], axis=0),
                                       jnp.concatenate([c, c, d, d], axis=0), NT), pt, rt, qt, kt)
    x_all = pm(lambda a, b, s: _mm3(jnp.concatenate([a, b], axis=0), s, NT), pt, rt, S)
    a_pq = pm(lambda a: jnp.where(bd_strict, a[:c2, :c2], 0.0), a_all)
    a_pk = pm(lambda a: jnp.where(bd_strict, a[:c2, c2:], 0.0), a_all)
    a_rq = pm(lambda a: jnp.where(sel_incl, pick(a[c2:, :c2]), 0.0), a_all)
    a_rk = pm(lambda a: jnp.where(sel_incl, pick(a[c2:, c2:]), 0.0), a_all)
    v_st = pm(stack, v)
    w0 = pm(_mm3, a_pk, v_st)

    t_acc = pm(lambda a: eye + a, a_pq)
    pw = pm(lambda a: _mm3(a, a), a_pq)
    span = 2
    while span * 2 < CHUNK:
        both = pm(lambda a, t: _mm3(a, jnp.concatenate([a, t], axis=1)), pw, t_acc)
        pw = pm(lambda b: b[:, :c2], both)
        t_acc = pm(lambda t, b: t + b[:, c2:], t_acc, both)
        span *= 2
    t_acc = pm(lambda t, a: t + _mm3(a, t), t_acc, pw)

    u_st = pm(lambda t, x, w: _mm3(t, stack(x[:CHUNK]) + w), t_acc, x_all, w0)
    o = pm(lambda x, aq, ak, u, vs: x[CHUNK:] + _mm3(jnp.concatenate([aq, ak], axis=1),
                                                      jnp.concatenate([u, vs], axis=0)),
           x_all, a_rq, a_rk, u_st, v_st)
    upd = pm(lambda u, vv, a, b: _mm3(jnp.concatenate([u[:CHUNK] + u[CHUNK:], vv], axis=0).T,
                                      jnp.concatenate([a, b], axis=0)), u_st, v, qb, kb)
    s_new = pm(lambda s, ce, u: s * jnp.exp(ce) + jnp.where(bd_mask, u, 0.0), S, cum_end, upd)

    inv_n = 1.0 / HEAD
    mean = pm(lambda x: _mm2(x, ones_bd) * inv_n, o)
    oc = pm(lambda x, m: x - m, o, mean)
    var = pm(lambda x: _mm2(x * x, ones_bd) * inv_n, oc)
    bonus = pm(lambda a, b, c: _mm2(a * b * c, ones_bd), r, k, rk)
    y = pm(lambda x, s2, gg, gb, bo, vv, gt: (x * lax.rsqrt(s2 + GN_EPS) * gg + gb + bo * vv) * gt,
           oc, var, gng, gnb, bonus, v, g)
    return y, s_new


def _wkv_kernel(r_ref, lw_ref, k_ref, v_ref, p_ref, q_ref, g_ref, rk_ref, gng_ref, gnb_ref, s0_ref,
                y_ref, so_ref, s_scr, *, n_pairs, n_chunks):
    tb = pl.program_id(2)

    @pl.when(tb == 0)
    def _():
        s_scr[...] = s0_ref[...]

    lane = lax.broadcasted_iota(jnp.int32, (CHUNK, PAIR), 1)
    m0 = lane < HEAD
    c2 = 2 * CHUNK
    ri = lax.broadcasted_iota(jnp.int32, (c2, c2), 0)
    ci = lax.broadcasted_iota(jnp.int32, (c2, c2), 1)
    same = (ri // CHUNK) == (ci // CHUNK)
    bd_strict = same & ((ci % CHUNK) < (ri % CHUNK))
    bd_mask = (ri // HEAD) == (ci // HEAD)
    eye = jnp.where(ri == ci, 1.0, 0.0).astype(F32)
    ones_bd = jnp.where(bd_mask, 1.0, 0.0).astype(BF16)
    ti = lax.broadcasted_iota(jnp.int32, (CHUNK, c2), 0)
    si = lax.broadcasted_iota(jnp.int32, (CHUNK, c2), 1)
    sel_incl = (si % CHUNK) <= ti
    consts = (m0, bd_strict, sel_incl, bd_mask, eye, ones_bd)
    tr = lax.broadcasted_iota(jnp.int32, (CHUNK, CHUNK), 0)
    tc = lax.broadcasted_iota(jnp.int32, (CHUNK, CHUNK), 1)
    tri = jnp.where(tc <= tr, 1.0, 0.0).astype(BF16)

    def chunk_body(c, carry):
        rows = pl.ds(pl.multiple_of(c * CHUNK, CHUNK), CHUNK)
        lw_all = lw_ref[rows, :]
        lw_hi, lw_lo = _split(lw_all)
        cum_all = _dot(tri, lw_hi) + _dot(tri, lw_lo)
        lss = [slice(pr * PAIR, (pr + 1) * PAIR) for pr in range(n_pairs)]
        seq = lambda ref: [ref[rows, ls] for ls in lss]
        vec = lambda ref: [ref[:, ls] for ls in lss]
        y, s_new = _wkv_chunk(
            seq(r_ref), [lw_all[:, ls] for ls in lss], [cum_all[:, ls] for ls in lss], seq(k_ref), seq(v_ref),
            seq(p_ref), seq(q_ref), seq(g_ref), vec(rk_ref), vec(gng_ref), vec(gnb_ref),
            [s_scr[pr] for pr in range(n_pairs)], consts)
        for pr, ls in enumerate(lss):
            s_scr[pr] = s_new[pr]
            y_ref[rows, ls] = y[pr].astype(y_ref.dtype)
        return carry

    lax.fori_loop(0, n_chunks, chunk_body, 0)

    @pl.when(tb == pl.num_programs(2) - 1)
    def _():
        so_ref[...] = s_scr[...]


def wkv(r, lw, k, v, p, q, g, rk, gng, gnb, s0_bd, t_len):
    rows, d_b = r.shape
    n_streams = rows // t_len
    lanes = WKV_PAIRS * PAIR
    n_pairs = lanes // PAIR
    tb_len = _row_tile(t_len, 256)
    tbs = t_len // tb_len
    seq = pl.BlockSpec((tb_len, lanes), lambda s, l, t: (s * tbs + t, l))
    vec = pl.BlockSpec((1, lanes), lambda s, l, t: (0, l))
    st = pl.BlockSpec((None, n_pairs, PAIR, PAIR), lambda s, l, t: (s, l, 0, 0))
    return pl.pallas_call(
        functools.partial(_wkv_kernel, n_pairs=n_pairs, n_chunks=tb_len // CHUNK),
        grid=(n_streams, d_b // lanes, tbs),
        in_specs=[seq] * 7 + [vec] * 3 + [st],
        out_specs=[seq, st],
        out_shape=[
            jax.ShapeDtypeStruct((rows, d_b), BF16),
            jax.ShapeDtypeStruct(s0_bd.shape, F32),
        ],
        scratch_shapes=[pltpu.VMEM((n_pairs, PAIR, PAIR), F32)],
        compiler_params=_params(("parallel", "parallel", "arbitrary")),
        name="wkv",
    )(r, lw, k, v, p, q, g, rk, gng, gnb, s0_bd)


def _mix_kernel(ya_ref, yb_ref, wa_ref, wb_ref, ga_ref, gb_ref, o_ref):
    o_ref[...] = (ga_ref[...] * _dot(ya_ref[...], wa_ref[...])
                  + gb_ref[...] * _dot(yb_ref[...], wb_ref[...])).astype(o_ref.dtype)


def mix(ya, yb, w_pa, w_pb, gates, tn=1024):
    rows, d_a = ya.shape
    d_b = yb.shape[1]
    d = w_pa.shape[1]
    tm = _row_tile(rows, 512)
    nj = d // tn
    return pl.pallas_call(
        _mix_kernel,
        grid=(rows // tm, nj),
        in_specs=[
            pl.BlockSpec((tm, d_a), lambda i, j: (i, 0)),
            pl.BlockSpec((tm, d_b), lambda i, j: (i, 0)),
            pl.BlockSpec((d_a, tn), lambda i, j: (0, j)),
            pl.BlockSpec((d_b, tn), lambda i, j: (0, j)),
            pl.BlockSpec((tm, tn), lambda i, j: (i, j)),
            pl.BlockSpec((tm, tn), lambda i, j: (i, j + nj)),
        ],
        out_specs=pl.BlockSpec((tm, tn), lambda i, j: (i, j)),
        out_shape=jax.ShapeDtypeStruct((rows, d), BF16),
        compiler_params=_params(("parallel", "parallel")),
        name="mix",
    )(ya, yb, w_pa, w_pb, gates, gates)


def _matmul_res_kernel(x_ref, m_ref, w_ref, o_ref):
    o_ref[...] = x_ref[...] + _dot(m_ref[...], w_ref[...])


def matmul_res(x, m, w, tn=1024):
    rows, d = x.shape
    kdim = m.shape[1]
    tm = _row_tile(rows, 512)
    return pl.pallas_call(
        _matmul_res_kernel,
        grid=(rows // tm, d // tn),
        in_specs=[
            pl.BlockSpec((tm, tn), lambda i, j: (i, j)),
            pl.BlockSpec((tm, kdim), lambda i, j: (i, 0)),
            pl.BlockSpec((kdim, tn), lambda i, j: (0, j)),
        ],
        out_specs=pl.BlockSpec((tm, tn), lambda i, j: (i, j)),
        out_shape=jax.ShapeDtypeStruct((rows, d), F32),
        compiler_params=_params(("parallel", "parallel")),
        name="matmul_res",
    )(x, m, w)


def _ffn_kernel(x_ref, g_ref, wu_ref, wd_ref, gf_ref, o_ref, h_ref, acc_ref, *, final_norm):
    f = pl.program_id(1)

    @pl.when(f == 0)
    def _():
        h_ref[...] = _rms(x_ref[...], g_ref[...]).astype(BF16)
        acc_ref[...] = jnp.zeros_like(acc_ref)

    up = jnp.maximum(_dot(h_ref[...], wu_ref[...]), 0.0)
    acc_ref[...] += _dot((up * up).astype(BF16), wd_ref[...])

    @pl.when(f == pl.num_programs(1) - 1)
    def _():
        y = x_ref[...] + acc_ref[...]
        if final_norm:
            y = _rms(y, gf_ref[...])
        o_ref[...] = y


def ffn(x, gain, w_up, w_down, final_gain, final_norm, tf=1024):
    rows, d = x.shape
    d_ff = w_up.shape[1]
    tm = _row_tile(rows, 512)
    return pl.pallas_call(
        functools.partial(_ffn_kernel, final_norm=final_norm),
        grid=(rows // tm, d_ff // tf),
        in_specs=[
            pl.BlockSpec((tm, d), lambda i, f: (i, 0)),
            pl.BlockSpec((1, d), lambda i, f: (0, 0)),
            pl.BlockSpec((d, tf), lambda i, f: (0, f)),
            pl.BlockSpec((tf, d), lambda i, f: (f, 0)),
            pl.BlockSpec((1, d), lambda i, f: (0, 0)),
        ],
        out_specs=pl.BlockSpec((tm, d), lambda i, f: (i, 0)),
        out_shape=jax.ShapeDtypeStruct((rows, d), F32),
        scratch_shapes=[pltpu.VMEM((tm, d), BF16), pltpu.VMEM((tm, d), F32)],
        compiler_params=_params(("parallel", "arbitrary")),
        name="ffn",
    )(x, gain, w_up, w_down, final_gain)


def _to_block_diag(s):
    b, h, n, _ = s.shape
    sr = s.reshape(b, h // 2, 2, n, n)
    z = jnp.zeros_like(sr[:, :, 0])
    top = jnp.concatenate([sr[:, :, 0], z], axis=-1)
    bot = jnp.concatenate([z, sr[:, :, 1]], axis=-1)
    return jnp.concatenate([top, bot], axis=-2)


def _from_block_diag(sbd):
    b, hp, n2, _ = sbd.shape
    n = n2 // 2
    return jnp.stack([sbd[:, :, :n, :n], sbd[:, :, n:, n:]], axis=2).reshape(b, 2 * hp, n, n)


def _layer(x, shift0, s0, t_len, wt, final_gain, final_norm):
    rows = x.shape[0]
    n_streams = rows // t_len
    d_b = wt["w0"].shape[1]
    d_shift = shift0.shape[1]
    width = wt["w_shift"].shape[1]

    uv = norm_matmul(x, wt["norm1"], wt["w_uv"], _gelu, tn=1024)
    pj = norm_matmul(x, wt["norm1"], wt["w_shift"], _identity, tn=width // 3)
    gates = norm_matmul(x, wt["norm1"], wt["w_gate"], _sigmoid, tn=1024)

    blk = MLP_BLOCK if t_len % MLP_BLOCK == 0 else t_len
    ya, vn = gmlp(uv, wt["ln_g"], wt["ln_b"], wt["w_s"][:, :blk, :blk], wt["b_full"][:, :blk], blk)

    shift0p = jnp.pad(shift0, ((0, 0), (0, width - d_shift)))[:, None, :]
    r, lw, k, v, p, q, g = rwkv_prep(pj, shift0p, wt["mu"], wt["w0"], wt["a0"], wt["k_k"], wt["k_a"],
                                     wt["wwa"], wt["g2p"], wt["e1"], wt["e1t"], t_len)
    yb, s_bd = wkv(r, lw, k, v, p, q, g, wt["r_k"], wt["gn_g"], wt["gn_b"], _to_block_diag(s0), t_len)

    mixed = mix(ya, yb, wt["w_pa"], wt["w_pb"], gates)
    x = matmul_res(x, mixed, wt["w_o"])
    x = ffn(x, wt["norm2"], wt["w_up"], wt["w_down"], final_gain, final_norm)

    new_shift = pj.reshape(n_streams, t_len, width)[:, -1, :d_shift]
    return x, vn, new_shift, _from_block_diag(s_bd)


def kernel(x_prompt, x_sample, state_tshift, state_wkv, norm1, w_in, ln_v_g, ln_v_b, w_s, b_s, mu_shift,
           w0, w2, a0, a2, g2, k_k, k_a, r_k, gn_g, gn_b, w_gate, w_pa, w_pb, w_o, norm2, w_up, w_down,
           norm_f):
    depth = w_in.shape[0]
    bp, tp, d = x_prompt.shape
    bs, ts, _ = x_sample.shape
    d_a = ln_v_g.shape[1]
    d_b = w0.shape[1]
    d_shift = mu_shift.shape[1]
    heads = d_b // HEAD
    assert tp % CHUNK == 0 and ts % CHUNK == 0 and heads % 4 == 0
    assert all(t % MLP_BLOCK == 0 or (t < MLP_BLOCK and t % 8 == 0) for t in (tp, ts))
    lo_w, lo_a = w2.shape[1], a2.shape[1]
    assert lo_w == HEAD and lo_a == HEAD
    lo_g = g2.shape[1]
    width = -(-d_shift // 384) * 384
    g_w = width - (3 * d_b + PAIR)
    dg = d_a // w_s.shape[1]

    head_of = jnp.arange(d_b) // HEAD
    e1 = (head_of[:, None] == jnp.arange(PAIR)[None, :]).astype(BF16)
    row = lambda a: a.reshape(1, -1)

    xp = x_prompt.reshape(bp * tp, d)
    xs = x_sample.reshape(bs * ts, d)
    zero_shift = jnp.zeros((bp, d_shift), F32)
    zero_wkv = jnp.zeros((bp, heads, HEAD, HEAD), F32)
    outs = {n: [] for n in ("tsh_p", "wkv_p", "tsh_s", "wkv_s", "vrow_s")}
    for l in range(depth):
        wwa = jnp.zeros((PAIR, 2 * d_b), F32)
        wwa = wwa.at[:lo_w, :d_b].set(w2[l]).at[lo_w:, d_b:].set(a2[l])
        wt = {
            "norm1": row(norm1[l]), "norm2": row(norm2[l]),
            "w_uv": w_in[l][:, :2 * d_a].astype(BF16),
            "w_shift": jnp.pad(w_in[l][:, 2 * d_a:], ((0, 0), (0, width - d_shift))).astype(BF16),
            "w_gate": w_gate[l].astype(BF16),
            "ln_g": row(ln_v_g[l]), "ln_b": row(ln_v_b[l]),
            "w_s": w_s[l],
            "b_full": jnp.broadcast_to(b_s[l][:, :, None], b_s[l].shape + (dg,)),
            "mu": row(jnp.pad(mu_shift[l], (0, width - d_shift))),
            "w0": row(w0[l]), "a0": row(a0[l]), "k_k": row(k_k[l]), "k_a": row(k_a[l]),
            "wwa": wwa,
            "g2p": jnp.pad(g2[l], ((0, g_w - lo_g), (0, 0))),
            "e1": e1, "e1t": e1.T,
            "r_k": row(r_k[l]), "gn_g": row(gn_g[l]), "gn_b": row(gn_b[l]),
            "w_pa": w_pa[l].astype(BF16), "w_pb": w_pb[l].astype(BF16), "w_o": w_o[l].astype(BF16),
            "w_up": w_up[l].astype(BF16), "w_down": w_down[l].astype(BF16),
        }
        last = l == depth - 1
        gf = row(norm_f)
        xp, _, shp, wkp = _layer(xp, zero_shift, zero_wkv, tp, wt, gf, last)
        xs, vrows, shs, wks = _layer(xs, state_tshift[l], state_wkv[l], ts, wt, gf, last)
        outs["tsh_p"].append(shp)
        outs["wkv_p"].append(wkp)
        outs["tsh_s"].append(shs)
        outs["wkv_s"].append(wks)
        outs["vrow_s"].append(vrows.reshape(bs, ts, d_a))
    return (xp.reshape(bp, tp, d), xs.reshape(bs, ts, d), jnp.stack(outs["tsh_p"]), jnp.stack(outs["wkv_p"]),
            jnp.stack(outs["tsh_s"]), jnp.stack(outs["wkv_s"]), jnp.stack(outs["vrow_s"]))
```

```python
import functools

import jax
import jax.numpy as jnp
from jax import lax
from jax.experimental import pallas as pl
from jax.experimental.pallas import tpu as pltpu

F32 = jnp.float32
BF16 = jnp.bfloat16

HEAD = 64
PAIR = 2 * HEAD
CHUNK = 64
WKV_PAIRS = 8
MLP_BLOCK = 128
PROJ_TN = 512
RMS_EPS = 1e-5
LN_EPS = 1e-5
GN_EPS = 64e-5
DECAY_SCALE = 0.6065306597126334
VMEM_LIMIT = 56 * 1024 * 1024

NN = (((1,), (0,)), ((), ()))
NT = (((1,), (1,)), ((), ()))


def _params(sem):
    return pltpu.CompilerParams(dimension_semantics=sem, vmem_limit_bytes=VMEM_LIMIT)


def _dot(a, b, dims=NN):
    return lax.dot_general(a, b, dims, preferred_element_type=F32)


def _split(x):
    hi = x.astype(BF16)
    lo = (x - hi.astype(F32)).astype(BF16)
    return hi, lo


def _mm3(a, b, dims=NN):
    ah, al = _split(a)
    bh, bl = _split(b)
    return _dot(ah, bh, dims) + (_dot(ah, bl, dims) + _dot(al, bh, dims))


def _mm2(a, b_exact):
    ah, al = _split(a)
    return _dot(ah, b_exact) + _dot(al, b_exact)


def _rms(x, g):
    return x * lax.rsqrt(jnp.mean(x * x, axis=-1, keepdims=True) + RMS_EPS) * g


def _gelu(x):
    return 0.5 * x * (1.0 + jnp.tanh(0.7978845608028654 * (x + 0.044715 * (x * x * x))))


def _sigmoid(x):
    return 1.0 / (1.0 + jnp.exp(-x))


def _identity(x):
    return x


def _row_tile(rows, cap):
    t = min(rows, cap)
    while rows % t:
        t //= 2
    return t


def _proj_kernel(x_ref, g_ref, w_ref, uv_ref, pj_ref, gt_ref, h_ref, *, n_uv, n_pj):
    j = pl.program_id(1)

    @pl.when(j == 0)
    def _():
        h_ref[...] = _rms(x_ref[...], g_ref[...]).astype(BF16)

    y = _dot(h_ref[...], w_ref[...])

    @pl.when(j < n_uv)
    def _():
        uv_ref[...] = _gelu(y)

    @pl.when((j >= n_uv) & (j < n_uv + n_pj))
    def _():
        pj_ref[...] = y

    @pl.when(j >= n_uv + n_pj)
    def _():
        gt_ref[...] = _sigmoid(y).astype(gt_ref.dtype)


def proj(x, gain, w_cat, n_uv_cols, n_pj_cols):
    rows, d = x.shape
    n = w_cat.shape[1]
    tn = PROJ_TN
    n_uv, n_pj = n_uv_cols // tn, n_pj_cols // tn
    n_gt = n // tn - n_uv - n_pj
    tm = _row_tile(rows, 1024)
    return pl.pallas_call(
        functools.partial(_proj_kernel, n_uv=n_uv, n_pj=n_pj),
        grid=(rows // tm, n // tn),
        in_specs=[
            pl.BlockSpec((tm, d), lambda i, j: (i, 0)),
            pl.BlockSpec((1, d), lambda i, j: (0, 0)),
            pl.BlockSpec((d, tn), lambda i, j: (0, j)),
        ],
        out_specs=[
            pl.BlockSpec((tm, tn), lambda i, j: (i, jnp.minimum(j, n_uv - 1))),
            pl.BlockSpec((tm, tn), lambda i, j: (i, jnp.clip(j - n_uv, 0, n_pj - 1))),
            pl.BlockSpec((tm, tn), lambda i, j: (i, jnp.clip(j - n_uv - n_pj, 0, n_gt - 1))),
        ],
        out_shape=[
            jax.ShapeDtypeStruct((rows, n_uv * tn), F32),
            jax.ShapeDtypeStruct((rows, n_pj * tn), F32),
            jax.ShapeDtypeStruct((rows, n_gt * tn), BF16),
        ],
        scratch_shapes=[pltpu.VMEM((tm, d), BF16)],
        compiler_params=_params(("parallel", "arbitrary")),
        name="proj",
    )(x, gain, w_cat)


def _gmlp_kernel(u_ref, v_ref, lg_ref, lb_ref, ws_ref, bs_ref, ya_ref, *maybe_vn_ref, blk, groups):
    v = v_ref[...]
    mu = jnp.mean(v, axis=-1, keepdims=True)
    vc = v - mu
    var = jnp.mean(vc * vc, axis=-1, keepdims=True)
    vn = vc * lax.rsqrt(var + LN_EPS) * lg_ref[...] + lb_ref[...]
    for vn_ref in maybe_vn_ref:
        vn_ref[...] = vn
    vnb = vn.astype(BF16)
    q_idx = lax.broadcasted_iota(jnp.int32, (blk, blk), 0)
    k_idx = lax.broadcasted_iota(jnp.int32, (blk, blk), 1)
    causal = k_idx <= q_idx
    dg = v.shape[1] // groups
    for g in range(groups):
        wg = jnp.where(causal, ws_ref[g], 0.0).astype(BF16)
        bg = bs_ref[g]
        for c in range(v.shape[0] // blk):
            rs = slice(c * blk, (c + 1) * blk)
            cs = slice(g * dg, (g + 1) * dg)
            s = _dot(wg, vnb[rs, cs]) + bg
            ya_ref[rs, cs] = (u_ref[rs, cs] * s).astype(ya_ref.dtype)


def gmlp(uv, ln_g, ln_b, ws, bs_full, blk, want_vn):
    rows = uv.shape[0]
    d_a = uv.shape[1] // 2
    groups = ws.shape[0]
    tm = _row_tile(rows, max(blk, 256))
    n_out = 2 if want_vn else 1
    return pl.pallas_call(
        functools.partial(_gmlp_kernel, blk=blk, groups=groups),
        grid=(rows // tm,),
        in_specs=[
            pl.BlockSpec((tm, d_a), lambda i: (i, 0)),
            pl.BlockSpec((tm, d_a), lambda i: (i, 1)),
            pl.BlockSpec((1, d_a), lambda i: (0, 0)),
            pl.BlockSpec((1, d_a), lambda i: (0, 0)),
            pl.BlockSpec((groups, blk, blk), lambda i: (0, 0, 0)),
            pl.BlockSpec((groups, blk, d_a // groups), lambda i: (0, 0, 0)),
        ],
        out_specs=[pl.BlockSpec((tm, d_a), lambda i: (i, 0))] * n_out,
        out_shape=[
            jax.ShapeDtypeStruct((rows, d_a), BF16),
            jax.ShapeDtypeStruct((rows, d_a), F32),
        ][:n_out],
        compiler_params=_params(("parallel",)),
        name="gmlp",
    )(uv, uv, ln_g, ln_b, ws, bs_full)


def _prep_kernel(p_ref, halo_ref, s0_ref, mu_ref, w0_ref, a0_ref, kk_ref, ka_ref, wwa_ref, g2_ref,
                 e1_ref, e1t_ref, r_o, lw_o, k_o, v_o, p_o, q_o, g_o, *, blocks_per_stream, d_b):
    i = pl.program_id(0)
    pb = p_ref[...]
    first = jnp.where(i % blocks_per_stream == 0, s0_ref[...], halo_ref[7:8, :])
    row = lax.broadcasted_iota(jnp.int32, pb.shape, 0)
    prev = jnp.where(row == 0, first, pltpu.roll(pb, 1, 0))
    xs = pb + (prev - pb) * mu_ref[...]

    r = xs[:, 0:d_b]
    k = xs[:, d_b:2 * d_b]
    v = xs[:, 2 * d_b:3 * d_b]
    wa = xs[:, 3 * d_b:3 * d_b + PAIR]
    gl = xs[:, 3 * d_b + PAIR:]
    lane = lax.broadcasted_iota(jnp.int32, wa.shape, 1)
    lora = _mm3(jnp.where(lane < HEAD, jnp.tanh(wa), wa), wwa_ref[...])
    lw = -DECAY_SCALE * _sigmoid(w0_ref[...] + lora[:, :d_b])
    a = _sigmoid(a0_ref[...] + lora[:, d_b:])
    g = _mm3(_sigmoid(gl), g2_ref[...])

    kk = k * kk_ref[...]
    ss = _mm2(_mm2(kk * kk, e1_ref[...]), e1t_ref[...])
    kk = kk * lax.rsqrt(jnp.maximum(ss, 1e-24))

    r_o[...] = r
    lw_o[...] = lw
    k_o[...] = k * (1.0 + (a - 1.0) * ka_ref[...])
    v_o[...] = v
    p_o[...] = kk
    q_o[...] = -(kk * a)
    g_o[...] = g


def rwkv_prep(p, shift0, mu, w0, a0, k_k, k_a, wwa, g2p, e1, e1t, t_len):
    rows, width = p.shape
    d_b = w0.shape[1]
    tm = _row_tile(t_len, 256)
    bps = t_len // tm
    vec = lambda n: pl.BlockSpec((1, n), lambda i: (0, 0))
    full = lambda a: pl.BlockSpec(a.shape, lambda i: (0, 0))
    out = pl.BlockSpec((tm, d_b), lambda i: (i, 0))
    return pl.pallas_call(
        functools.partial(_prep_kernel, blocks_per_stream=bps, d_b=d_b),
        grid=(rows // tm,),
        in_specs=[
            pl.BlockSpec((tm, width), lambda i: (i, 0)),
            pl.BlockSpec((8, width), lambda i: (jnp.maximum(i * (tm // 8) - 1, 0), 0)),
            pl.BlockSpec((None, 1, width), lambda i: (i // bps, 0, 0)),
            vec(width), vec(d_b), vec(d_b), vec(d_b), vec(d_b),
            full(wwa), full(g2p), full(e1), full(e1t),
        ],
        out_specs=[out] * 7,
        out_shape=[jax.ShapeDtypeStruct((rows, d_b), F32)] * 7,
        compiler_params=_params(("parallel",)),
        name="rwkv_prep",
    )(p, p, shift0, mu, w0, a0, k_k, k_a, wwa, g2p, e1, e1t)


def _wkv_chunk(r, lw, cum, k, v, p, q, g, rk, gng, gnb, S, consts):
    m0, bd_strict, sel_incl, bd_mask, eye, ones_bd = consts
    c2 = 2 * CHUNK
    pm = lambda f, *ls: [f(*a) for a in zip(*ls)]
    zero = jnp.zeros((CHUNK, PAIR), F32)
    stack = lambda x: jnp.concatenate([jnp.where(m0, x, zero), jnp.where(m0, zero, x)], axis=0)
    pick = lambda blk: jnp.where(m0, blk[:CHUNK], blk[CHUNK:])

    cum_end = pm(lambda c: c[CHUNK - 1:CHUNK, :], cum)
    g_in = pm(jnp.exp, cum)
    g_ex = pm(lambda c, l: jnp.exp(c - l), cum, lw)
    g_inv = pm(lambda c: jnp.exp(-c), cum)
    g_bar = pm(lambda ce, c: jnp.exp(ce - c), cum_end, cum)
    mul = lambda a, b: a * b
    pt, rt = pm(mul, p, g_ex), pm(mul, r, g_in)
    qt, kt = pm(mul, q, g_inv), pm(mul, k, g_inv)
    qb, kb = pm(mul, q, g_bar), pm(mul, k, g_bar)

    a_all = pm(lambda a, b, c, d: _mm3(jnp.concatenate([stack(a), stack(b)], axis=0),
                                       jnp.concatenate([c, c, d, d], axis=0), NT), pt, rt, qt, kt)
    x_all = pm(lambda a, b, s: _mm3(jnp.concatenate([a, b], axis=0), s, NT), pt, rt, S)
    a_pq = pm(lambda a: jnp.where(bd_strict, a[:c2, :c2], 0.0), a_all)
    a_pk = pm(lambda a: jnp.where(bd_strict, a[:c2, c2:], 0.0), a_all)
    a_rq = pm(lambda a: jnp.where(sel_incl, pick(a[c2:, :c2]), 0.0), a_all)
    a_rk = pm(lambda a: jnp.where(sel_incl, pick(a[c2:, c2:]), 0.0), a_all)
    v_st = pm(stack, v)
    w0 = pm(_mm3, a_pk, v_st)

    t_acc = pm(lambda a: eye + a, a_pq)
    pw = pm(lambda a: _mm3(a, a), a_pq)
    span = 2
    while span * 2 < CHUNK:
        both = pm(lambda a, t: _mm3(a, jnp.concatenate([a, t], axis=1)), pw, t_acc)
        pw = pm(lambda b: b[:, :c2], both)
        t_acc = pm(lambda t, b: t + b[:, c2:], t_acc, both)
        span *= 2
    t_acc = pm(lambda t, a: t + _mm3(a, t), t_acc, pw)

    u_st = pm(lambda t, x, w: _mm3(t, stack(x[:CHUNK]) + w), t_acc, x_all, w0)
    o = pm(lambda x, aq, ak, u, vs: x[CHUNK:] + _mm3(jnp.concatenate([aq, ak], axis=1),
                                                      jnp.concatenate([u, vs], axis=0)),
           x_all, a_rq, a_rk, u_st, v_st)
    upd = pm(lambda u, vv, a, b: _mm3(jnp.concatenate([u[:CHUNK] + u[CHUNK:], vv], axis=0).T,
                                      jnp.concatenate([a, b], axis=0)), u_st, v, qb, kb)
    s_new = pm(lambda s, ce, u: s * jnp.exp(ce) + jnp.where(bd_mask, u, 0.0), S, cum_end, upd)

    inv_n = 1.0 / HEAD
    mean = pm(lambda x: _mm2(x, ones_bd) * inv_n, o)
    oc = pm(lambda x, m: x - m, o, mean)
    var = pm(lambda x: _mm2(x * x, ones_bd) * inv_n, oc)
    bonus = pm(lambda a, b, c: _mm2(a * b * c, ones_bd), r, k, rk)
    y = pm(lambda x, s2, gg, gb, bo, vv, gt: (x * lax.rsqrt(s2 + GN_EPS) * gg + gb + bo * vv) * gt,
           oc, var, gng, gnb, bonus, v, g)
    return y, s_new


def _wkv_kernel(r_ref, lw_ref, k_ref, v_ref, p_ref, q_ref, g_ref, rk_ref, gng_ref, gnb_ref, s0_ref,
                y_ref, so_ref, s_scr, *, n_pairs, n_chunks):
    tb = pl.program_id(2)

    @pl.when(tb == 0)
    def _():
        s_scr[...] = s0_ref[...]

    lane = lax.broadcasted_iota(jnp.int32, (CHUNK, PAIR), 1)
    m0 = lane < HEAD
    c2 = 2 * CHUNK
    ri = lax.broadcasted_iota(jnp.int32, (c2, c2), 0)
    ci = lax.broadcasted_iota(jnp.int32, (c2, c2), 1)
    same = (ri // CHUNK) == (ci // CHUNK)
    bd_strict = same & ((ci % CHUNK) < (ri % CHUNK))
    bd_mask = (ri // HEAD) == (ci // HEAD)
    eye = jnp.where(ri == ci, 1.0, 0.0).astype(F32)
    ones_bd = jnp.where(bd_mask, 1.0, 0.0).astype(BF16)
    ti = lax.broadcasted_iota(jnp.int32, (CHUNK, c2), 0)
    si = lax.broadcasted_iota(jnp.int32, (CHUNK, c2), 1)
    sel_incl = (si % CHUNK) <= ti
    consts = (m0, bd_strict, sel_incl, bd_mask, eye, ones_bd)
    tr = lax.broadcasted_iota(jnp.int32, (CHUNK, CHUNK), 0)
    tc = lax.broadcasted_iota(jnp.int32, (CHUNK, CHUNK), 1)
    tri = jnp.where(tc <= tr, 1.0, 0.0).astype(BF16)

    def chunk_body(c, carry):
        rows = pl.ds(pl.multiple_of(c * CHUNK, CHUNK), CHUNK)
        lw_all = lw_ref[rows, :]
        lw_hi, lw_lo = _split(lw_all)
        cum_all = _dot(tri, lw_hi) + _dot(tri, lw_lo)
        lss = [slice(pr * PAIR, (pr + 1) * PAIR) for pr in range(n_pairs)]
        seq = lambda ref: [ref[rows, ls] for ls in lss]
        vec = lambda ref: [ref[:, ls] for ls in lss]
        y, s_new = _wkv_chunk(
            seq(r_ref), [lw_all[:, ls] for ls in lss], [cum_all[:, ls] for ls in lss], seq(k_ref), seq(v_ref),
            seq(p_ref), seq(q_ref), seq(g_ref), vec(rk_ref), vec(gng_ref), vec(gnb_ref),
            [s_scr[pr] for pr in range(n_pairs)], consts)
        for pr, ls in enumerate(lss):
            s_scr[pr] = s_new[pr]
            y_ref[rows, ls] = y[pr].astype(y_ref.dtype)
        return carry

    lax.fori_loop(0, n_chunks, chunk_body, 0)

    @pl.when(tb == pl.num_programs(2) - 1)
    def _():
        so_ref[...] = s_scr[...]


def wkv(r, lw, k, v, p, q, g, rk, gng, gnb, s0_bd, t_len):
    rows, d_b = r.shape
    n_streams = rows // t_len
    lanes = WKV_PAIRS * PAIR
    n_pairs = lanes // PAIR
    tb_len = _row_tile(t_len, 256)
    tbs = t_len // tb_len
    seq = pl.BlockSpec((tb_len, lanes), lambda s, l, t: (s * tbs + t, l))
    vec = pl.BlockSpec((1, lanes), lambda s, l, t: (0, l))
    st = pl.BlockSpec((None, n_pairs, PAIR, PAIR), lambda s, l, t: (s, l, 0, 0))
    return pl.pallas_call(
        functools.partial(_wkv_kernel, n_pairs=n_pairs, n_chunks=tb_len // CHUNK),
        grid=(n_streams, d_b // lanes, tbs),
        in_specs=[seq] * 7 + [vec] * 3 + [st],
        out_specs=[seq, st],
        out_shape=[
            jax.ShapeDtypeStruct((rows, d_b), BF16),
            jax.ShapeDtypeStruct(s0_bd.shape, F32),
        ],
        scratch_shapes=[pltpu.VMEM((n_pairs, PAIR, PAIR), F32)],
        compiler_params=_params(("parallel", "parallel", "arbitrary")),
        name="wkv",
    )(r, lw, k, v, p, q, g, rk, gng, gnb, s0_bd)


def _mix_out_kernel(x_ref, ya_ref, yb_ref, ga_ref, gb_ref, wa_ref, wb_ref, wo_ref, o_ref):
    mixed = (ga_ref[...].astype(F32) * _dot(ya_ref[...], wa_ref[...])
             + gb_ref[...].astype(F32) * _dot(yb_ref[...], wb_ref[...]))
    o_ref[...] = x_ref[...] + _dot(mixed.astype(BF16), wo_ref[...])


def mix_out(x, ya, yb, gates, w_pa, w_pb, w_o):
    rows, d = x.shape
    d_a, d_b = ya.shape[1], yb.shape[1]
    tm = _row_tile(rows, 256)
    resident = lambda a: pl.BlockSpec(a.shape, lambda i: (0, 0), pipeline_mode=pl.Buffered(1))
    return pl.pallas_call(
        _mix_out_kernel,
        grid=(rows // tm,),
        in_specs=[
            pl.BlockSpec((tm, d), lambda i: (i, 0)),
            pl.BlockSpec((tm, d_a), lambda i: (i, 0)),
            pl.BlockSpec((tm, d_b), lambda i: (i, 0)),
            pl.BlockSpec((tm, d), lambda i: (i, 0)),
            pl.BlockSpec((tm, d), lambda i: (i, 1)),
            resident(w_pa), resident(w_pb), resident(w_o),
        ],
        out_specs=pl.BlockSpec((tm, d), lambda i: (i, 0)),
        out_shape=jax.ShapeDtypeStruct((rows, d), F32),
        compiler_params=_params(("parallel",)),
        name="mix_out",
    )(x, ya, yb, gates, gates, w_pa, w_pb, w_o)


def _ffn_kernel(x_ref, g_ref, wu_ref, wd_ref, gf_ref, o_ref, h_ref, acc_ref, *, final_norm):
    f = pl.program_id(1)

    @pl.when(f == 0)
    def _():
        h_ref[...] = _rms(x_ref[...], g_ref[...]).astype(BF16)
        acc_ref[...] = jnp.zeros_like(acc_ref)

    up = jnp.maximum(_dot(h_ref[...], wu_ref[...]), 0.0)
    acc_ref[...] += _dot((up * up).astype(BF16), wd_ref[...])

    @pl.when(f == pl.num_programs(1) - 1)
    def _():
        y = x_ref[...] + acc_ref[...]
        if final_norm:
            y = _rms(y, gf_ref[...])
        o_ref[...] = y


def ffn(x, gain, w_up, w_down, final_gain, final_norm, tf=1024):
    rows, d = x.shape
    d_ff = w_up.shape[1]
    tm = _row_tile(rows, 512)
    return pl.pallas_call(
        functools.partial(_ffn_kernel, final_norm=final_norm),
        grid=(rows // tm, d_ff // tf),
        in_specs=[
            pl.BlockSpec((tm, d), lambda i, f: (i, 0)),
            pl.BlockSpec((1, d), lambda i, f: (0, 0)),
            pl.BlockSpec((d, tf), lambda i, f: (0, f)),
            pl.BlockSpec((tf, d), lambda i, f: (f, 0)),
            pl.BlockSpec((1, d), lambda i, f: (0, 0)),
        ],
        out_specs=pl.BlockSpec((tm, d), lambda i, f: (i, 0)),
        out_shape=jax.ShapeDtypeStruct((rows, d), F32),
        scratch_shapes=[pltpu.VMEM((tm, d), BF16), pltpu.VMEM((tm, d), F32)],
        compiler_params=_params(("parallel", "arbitrary")),
        name="ffn",
    )(x, gain, w_up, w_down, final_gain)


def _to_block_diag(s):
    b, h, n, _ = s.shape
    sr = s.reshape(b, h // 2, 2, n, n)
    z = jnp.zeros_like(sr[:, :, 0])
    top = jnp.concatenate([sr[:, :, 0], z], axis=-1)
    bot = jnp.concatenate([z, sr[:, :, 1]], axis=-1)
    return jnp.concatenate([top, bot], axis=-2)


def _from_block_diag(sbd):
    b, hp, n2, _ = sbd.shape
    n = n2 // 2
    return jnp.stack([sbd[:, :, :n, :n], sbd[:, :, n:, n:]], axis=2).reshape(b, 2 * hp, n, n)


def _layer(x, shift0, s0, t_len, wt, final_gain, final_norm, want_vn):
    rows = x.shape[0]
    n_streams = rows // t_len
    d_a = wt["ln_g"].shape[1]
    d_shift = shift0.shape[1]
    width = wt["mu"].shape[1]

    uv, pj, gates = proj(x, wt["norm1"], wt["w_cat"], 2 * d_a, width)

    blk = MLP_BLOCK if t_len % MLP_BLOCK == 0 else t_len
    ya, *vn = gmlp(uv, wt["ln_g"], wt["ln_b"], wt["w_s"][:, :blk, :blk], wt["b_full"][:, :blk], blk, want_vn)

    shift0p = jnp.pad(shift0, ((0, 0), (0, width - d_shift)))[:, None, :]
    r, lw, k, v, p, q, g = rwkv_prep(pj, shift0p, wt["mu"], wt["w0"], wt["a0"], wt["k_k"], wt["k_a"],
                                     wt["wwa"], wt["g2p"], wt["e1"], wt["e1t"], t_len)
    yb, s_bd = wkv(r, lw, k, v, p, q, g, wt["r_k"], wt["gn_g"], wt["gn_b"], _to_block_diag(s0), t_len)

    x = mix_out(x, ya, yb, gates, wt["w_pa"], wt["w_pb"], wt["w_o"])
    x = ffn(x, wt["norm2"], wt["w_up"], wt["w_down"], final_gain, final_norm)

    new_shift = pj.reshape(n_streams, t_len, width)[:, -1, :d_shift]
    return x, (vn[0] if want_vn else None), new_shift, _from_block_diag(s_bd)


def kernel(x_prompt, x_sample, state_tshift, state_wkv, norm1, w_in, ln_v_g, ln_v_b, w_s, b_s, mu_shift,
           w0, w2, a0, a2, g2, k_k, k_a, r_k, gn_g, gn_b, w_gate, w_pa, w_pb, w_o, norm2, w_up, w_down,
           norm_f):
    depth = w_in.shape[0]
    bp, tp, d = x_prompt.shape
    bs, ts, _ = x_sample.shape
    d_a = ln_v_g.shape[1]
    d_b = w0.shape[1]
    d_shift = mu_shift.shape[1]
    heads = d_b // HEAD
    assert tp % CHUNK == 0 and ts % CHUNK == 0 and heads % 4 == 0
    assert all(t % MLP_BLOCK == 0 or (t < MLP_BLOCK and t % 8 == 0) for t in (tp, ts))
    lo_w, lo_a = w2.shape[1], a2.shape[1]
    assert lo_w == HEAD and lo_a == HEAD
    lo_g = g2.shape[1]
    width = -(-d_shift // PROJ_TN) * PROJ_TN
    g_w = width - (3 * d_b + PAIR)
    dg = d_a // w_s.shape[1]

    head_of = jnp.arange(d_b) // HEAD
    e1 = (head_of[:, None] == jnp.arange(PAIR)[None, :]).astype(BF16)
    row = lambda a: a.reshape(1, -1)

    xp = x_prompt.reshape(bp * tp, d)
    xs = x_sample.reshape(bs * ts, d)
    zero_shift = jnp.zeros((bp, d_shift), F32)
    zero_wkv = jnp.zeros((bp, heads, HEAD, HEAD), F32)
    outs = {n: [] for n in ("tsh_p", "wkv_p", "tsh_s", "wkv_s", "vrow_s")}
    for l in range(depth):
        wwa = jnp.zeros((PAIR, 2 * d_b), F32)
        wwa = wwa.at[:lo_w, :d_b].set(w2[l]).at[lo_w:, d_b:].set(a2[l])
        wt = {
            "norm1": row(norm1[l]), "norm2": row(norm2[l]),
            "w_cat": jnp.concatenate(
                [w_in[l].astype(BF16), jnp.zeros((d, width - d_shift), BF16), w_gate[l].astype(BF16)], axis=1),
            "ln_g": row(ln_v_g[l]), "ln_b": row(ln_v_b[l]),
            "w_s": w_s[l],
            "b_full": jnp.broadcast_to(b_s[l][:, :, None], b_s[l].shape + (dg,)),
            "mu": row(jnp.pad(mu_shift[l], (0, width - d_shift))),
            "w0": row(w0[l]), "a0": row(a0[l]), "k_k": row(k_k[l]), "k_a": row(k_a[l]),
            "wwa": wwa,
            "g2p": jnp.pad(g2[l], ((0, g_w - lo_g), (0, 0))),
            "e1": e1, "e1t": e1.T,
            "r_k": row(r_k[l]), "gn_g": row(gn_g[l]), "gn_b": row(gn_b[l]),
            "w_pa": w_pa[l].astype(BF16), "w_pb": w_pb[l].astype(BF16), "w_o": w_o[l].astype(BF16),
            "w_up": w_up[l].astype(BF16), "w_down": w_down[l].astype(BF16),
        }
        last = l == depth - 1
        gf = row(norm_f)
        xp, _, shp, wkp = _layer(xp, zero_shift, zero_wkv, tp, wt, gf, last, False)
        xs, vrows, shs, wks = _layer(xs, state_tshift[l], state_wkv[l], ts, wt, gf, last, True)
        outs["tsh_p"].append(shp)
        outs["wkv_p"].append(wkp)
        outs["tsh_s"].append(shs)
        outs["wkv_s"].append(wks)
        outs["vrow_s"].append(vrows.reshape(bs, ts, d_a))
    return (xp.reshape(bp, tp, d), xs.reshape(bs, ts, d), jnp.stack(outs["tsh_p"]), jnp.stack(outs["wkv_p"]),
            jnp.stack(outs["tsh_s"]), jnp.stack(outs["wkv_s"]), jnp.stack(outs["vrow_s"]))
```

```python
import functools

import jax
import jax.numpy as jnp
from jax import lax
from jax.experimental import pallas as pl
from jax.experimental.pallas import tpu as pltpu

F32 = jnp.float32
BF16 = jnp.bfloat16

HEAD = 64
PAIR = 2 * HEAD
CHUNK = 64
WKV_PAIRS = 8
WKV_STREAMS = 2
MLP_BLOCK = 128
PROJ_TN = 512
RMS_EPS = 1e-5
LN_EPS = 1e-5
GN_EPS = 64e-5
DECAY_SCALE = 0.6065306597126334
VMEM_LIMIT = 56 * 1024 * 1024

NN = (((1,), (0,)), ((), ()))
NT = (((1,), (1,)), ((), ()))


def _params(sem):
    return pltpu.CompilerParams(dimension_semantics=sem, vmem_limit_bytes=VMEM_LIMIT)


def _dot(a, b, dims=NN):
    return lax.dot_general(a, b, dims, preferred_element_type=F32)


def _split(x):
    hi = x.astype(BF16)
    lo = (x - hi.astype(F32)).astype(BF16)
    return hi, lo


def _mm3(a, b, dims=NN):
    ah, al = _split(a)
    bh, bl = _split(b)
    return _dot(ah, bh, dims) + (_dot(ah, bl, dims) + _dot(al, bh, dims))


def _mm1(a, b, dims=NN):
    return _dot(a.astype(BF16), b.astype(BF16), dims)


def _mm2(a, b_exact):
    ah, al = _split(a)
    return _dot(ah, b_exact) + _dot(al, b_exact)


def _rms(x, g):
    return x * lax.rsqrt(jnp.mean(x * x, axis=-1, keepdims=True) + RMS_EPS) * g


def _gelu(x):
    return 0.5 * x * (1.0 + jnp.tanh(0.7978845608028654 * (x + 0.044715 * (x * x * x))))


def _sigmoid(x):
    return 1.0 / (1.0 + jnp.exp(-x))


def _identity(x):
    return x


def _row_tile(rows, cap):
    t = min(rows, cap)
    while rows % t:
        t //= 2
    return t


def _proj_kernel(x_ref, g_ref, w_ref, uv_ref, pj_ref, gt_ref, h_ref, *, n_uv, n_pj):
    j = pl.program_id(1)

    @pl.when(j == 0)
    def _():
        h_ref[...] = _rms(x_ref[...], g_ref[...]).astype(BF16)

    y = _dot(h_ref[...], w_ref[...])

    @pl.when(j < n_uv)
    def _():
        uv_ref[...] = _gelu(y)

    @pl.when((j >= n_uv) & (j < n_uv + n_pj))
    def _():
        pj_ref[...] = y

    @pl.when(j >= n_uv + n_pj)
    def _():
        gt_ref[...] = _sigmoid(y).astype(gt_ref.dtype)


def proj(x, gain, w_cat, n_uv_cols, n_pj_cols):
    rows, d = x.shape
    n = w_cat.shape[1]
    tn = PROJ_TN
    n_uv, n_pj = n_uv_cols // tn, n_pj_cols // tn
    n_gt = n // tn - n_uv - n_pj
    tm = _row_tile(rows, 1024)
    return pl.pallas_call(
        functools.partial(_proj_kernel, n_uv=n_uv, n_pj=n_pj),
        grid=(rows // tm, n // tn),
        in_specs=[
            pl.BlockSpec((tm, d), lambda i, j: (i, 0)),
            pl.BlockSpec((1, d), lambda i, j: (0, 0)),
            pl.BlockSpec((d, tn), lambda i, j: (0, j)),
        ],
        out_specs=[
            pl.BlockSpec((tm, tn), lambda i, j: (i, jnp.minimum(j, n_uv - 1))),
            pl.BlockSpec((tm, tn), lambda i, j: (i, jnp.clip(j - n_uv, 0, n_pj - 1))),
            pl.BlockSpec((tm, tn), lambda i, j: (i, jnp.clip(j - n_uv - n_pj, 0, n_gt - 1))),
        ],
        out_shape=[
            jax.ShapeDtypeStruct((rows, n_uv * tn), F32),
            jax.ShapeDtypeStruct((rows, n_pj * tn), F32),
            jax.ShapeDtypeStruct((rows, n_gt * tn), BF16),
        ],
        scratch_shapes=[pltpu.VMEM((tm, d), BF16)],
        compiler_params=_params(("parallel", "arbitrary")),
        name="proj",
    )(x, gain, w_cat)


def _gmlp_kernel(u_ref, v_ref, lg_ref, lb_ref, ws_ref, bs_ref, ya_ref, *maybe_vn_ref, blk, groups):
    v = v_ref[...]
    mu = jnp.mean(v, axis=-1, keepdims=True)
    vc = v - mu
    var = jnp.mean(vc * vc, axis=-1, keepdims=True)
    vn = vc * lax.rsqrt(var + LN_EPS) * lg_ref[...] + lb_ref[...]
    for vn_ref in maybe_vn_ref:
        vn_ref[...] = vn
    vnb = vn.astype(BF16)
    q_idx = lax.broadcasted_iota(jnp.int32, (blk, blk), 0)
    k_idx = lax.broadcasted_iota(jnp.int32, (blk, blk), 1)
    causal = k_idx <= q_idx
    dg = v.shape[1] // groups
    for g in range(groups):
        wg = jnp.where(causal, ws_ref[g], 0.0).astype(BF16)
        bg = bs_ref[g]
        for c in range(v.shape[0] // blk):
            rs = slice(c * blk, (c + 1) * blk)
            cs = slice(g * dg, (g + 1) * dg)
            s = _dot(wg, vnb[rs, cs]) + bg
            ya_ref[rs, cs] = (u_ref[rs, cs] * s).astype(ya_ref.dtype)


def gmlp(uv, ln_g, ln_b, ws, bs_full, blk, want_vn):
    rows = uv.shape[0]
    d_a = uv.shape[1] // 2
    groups = ws.shape[0]
    tm = _row_tile(rows, max(blk, 256))
    n_out = 2 if want_vn else 1
    return pl.pallas_call(
        functools.partial(_gmlp_kernel, blk=blk, groups=groups),
        grid=(rows // tm,),
        in_specs=[
            pl.BlockSpec((tm, d_a), lambda i: (i, 0)),
            pl.BlockSpec((tm, d_a), lambda i: (i, 1)),
            pl.BlockSpec((1, d_a), lambda i: (0, 0)),
            pl.BlockSpec((1, d_a), lambda i: (0, 0)),
            pl.BlockSpec((groups, blk, blk), lambda i: (0, 0, 0)),
            pl.BlockSpec((groups, blk, d_a // groups), lambda i: (0, 0, 0)),
        ],
        out_specs=[pl.BlockSpec((tm, d_a), lambda i: (i, 0))] * n_out,
        out_shape=[
            jax.ShapeDtypeStruct((rows, d_a), BF16),
            jax.ShapeDtypeStruct((rows, d_a), F32),
        ][:n_out],
        compiler_params=_params(("parallel",)),
        name="gmlp",
    )(uv, uv, ln_g, ln_b, ws, bs_full)


def _prep_kernel(p_ref, halo_ref, s0_ref, mu_ref, w0_ref, a0_ref, kk_ref, ka_ref, wwa_ref, g2_ref,
                 e1_ref, e1t_ref, r_o, lw_o, k_o, v_o, p_o, q_o, g_o, *, blocks_per_stream, d_b):
    i = pl.program_id(0)
    pb = p_ref[...]
    first = jnp.where(i % blocks_per_stream == 0, s0_ref[...], halo_ref[7:8, :])
    row = lax.broadcasted_iota(jnp.int32, pb.shape, 0)
    prev = jnp.where(row == 0, first, pltpu.roll(pb, 1, 0))
    xs = pb + (prev - pb) * mu_ref[...]

    r = xs[:, 0:d_b]
    k = xs[:, d_b:2 * d_b]
    v = xs[:, 2 * d_b:3 * d_b]
    wa = xs[:, 3 * d_b:3 * d_b + PAIR]
    gl = xs[:, 3 * d_b + PAIR:]
    lane = lax.broadcasted_iota(jnp.int32, wa.shape, 1)
    lora = _mm3(jnp.where(lane < HEAD, jnp.tanh(wa), wa), wwa_ref[...])
    lw = -DECAY_SCALE * _sigmoid(w0_ref[...] + lora[:, :d_b])
    a = _sigmoid(a0_ref[...] + lora[:, d_b:])
    g = _mm3(_sigmoid(gl), g2_ref[...])

    kk = k * kk_ref[...]
    ss = _mm2(_mm2(kk * kk, e1_ref[...]), e1t_ref[...])
    kk = kk * lax.rsqrt(jnp.maximum(ss, 1e-24))

    r_o[...] = r
    lw_o[...] = lw
    k_o[...] = k * (1.0 + (a - 1.0) * ka_ref[...])
    v_o[...] = v
    p_o[...] = kk
    q_o[...] = -(kk * a)
    g_o[...] = g


def rwkv_prep(p, shift0, mu, w0, a0, k_k, k_a, wwa, g2p, e1, e1t, t_len):
    rows, width = p.shape
    d_b = w0.shape[1]
    tm = _row_tile(t_len, 256)
    bps = t_len // tm
    vec = lambda n: pl.BlockSpec((1, n), lambda i: (0, 0))
    full = lambda a: pl.BlockSpec(a.shape, lambda i: (0, 0))
    out = pl.BlockSpec((tm, d_b), lambda i: (i, 0))
    return pl.pallas_call(
        functools.partial(_prep_kernel, blocks_per_stream=bps, d_b=d_b),
        grid=(rows // tm,),
        in_specs=[
            pl.BlockSpec((tm, width), lambda i: (i, 0)),
            pl.BlockSpec((8, width), lambda i: (jnp.maximum(i * (tm // 8) - 1, 0), 0)),
            pl.BlockSpec((None, 1, width), lambda i: (i // bps, 0, 0)),
            vec(width), vec(d_b), vec(d_b), vec(d_b), vec(d_b),
            full(wwa), full(g2p), full(e1), full(e1t),
        ],
        out_specs=[out] * 7,
        out_shape=[jax.ShapeDtypeStruct((rows, d_b), F32)] * 7,
        compiler_params=_params(("parallel",)),
        name="rwkv_prep",
    )(p, p, shift0, mu, w0, a0, k_k, k_a, wwa, g2p, e1, e1t)


def _wkv_chunk(r, lw, cum, k, v, p, q, g, rk, gng, gnb, S, consts):
    m0, bd_strict, sel_incl, bd_mask, eye, ones_bd, inv_masks = consts
    c2 = 2 * CHUNK
    pm = lambda f, *ls: [f(*a) for a in zip(*ls)]
    mm_state, mm_side = _mm1, _mm1
    zero = jnp.zeros((CHUNK, PAIR), F32)
    stack = lambda x: jnp.concatenate([jnp.where(m0, x, zero), jnp.where(m0, zero, x)], axis=0)
    pick = lambda blk: jnp.where(m0, blk[:CHUNK], blk[CHUNK:])

    cum_end = pm(lambda c: c[CHUNK - 1:CHUNK, :], cum)
    g_in = pm(jnp.exp, cum)
    g_ex = pm(lambda c, l: jnp.exp(c - l), cum, lw)
    g_inv = pm(lambda c: jnp.exp(-c), cum)
    g_bar = pm(lambda ce, c: jnp.exp(ce - c), cum_end, cum)
    mul = lambda a, b: a * b
    pt, rt = pm(mul, p, g_ex), pm(mul, r, g_in)
    qt, kt = pm(mul, q, g_inv), pm(mul, k, g_inv)
    qb, kb = pm(mul, q, g_bar), pm(mul, k, g_bar)

    a_all = pm(lambda a, b, c, d: mm_side(jnp.concatenate([stack(a), stack(b)], axis=0),
                                          jnp.concatenate([c, c, d, d], axis=0), NT), pt, rt, qt, kt)
    x_all = pm(lambda a, b, s: mm_state(jnp.concatenate([a, b], axis=0), s, NT), pt, rt, S)
    a_pq = pm(lambda a: jnp.where(bd_strict, a[:c2, :c2], 0.0), a_all)
    a_pk = pm(lambda a: jnp.where(bd_strict, a[:c2, c2:], 0.0), a_all)
    a_rq = pm(lambda a: jnp.where(sel_incl, pick(a[c2:, :c2]), 0.0), a_all)
    a_rk = pm(lambda a: jnp.where(sel_incl, pick(a[c2:, c2:]), 0.0), a_all)
    v_st = pm(stack, v)
    w0 = pm(mm_side, a_pk, v_st)

    t_acc = pm(lambda a: eye + jnp.where(inv_masks[0], a, 0.0), a_pq)
    for off in inv_masks[1:]:
        y = pm(lambda a, t: mm_side(jnp.where(off, a, 0.0), t), a_pq, t_acc)
        t_acc = pm(lambda t, yy: t + mm_side(t, yy), t_acc, y)

    u_st = pm(lambda t, x, w: mm_state(t, stack(x[:CHUNK]) + w), t_acc, x_all, w0)
    o = pm(lambda x, aq, ak, u, vs: x[CHUNK:] + mm_side(jnp.concatenate([aq, ak], axis=1),
                                                         jnp.concatenate([u, vs], axis=0)),
           x_all, a_rq, a_rk, u_st, v_st)
    upd = pm(lambda u, vv, a, b: mm_state(jnp.concatenate([u[:CHUNK] + u[CHUNK:], vv], axis=0).T,
                                          jnp.concatenate([a, b], axis=0)), u_st, v, qb, kb)
    s_new = pm(lambda s, ce, u: s * jnp.exp(ce) + jnp.where(bd_mask, u, 0.0), S, cum_end, upd)

    inv_n = 1.0 / HEAD
    mean = pm(lambda x: mm_side(x, ones_bd) * inv_n, o)
    oc = pm(lambda x, m: x - m, o, mean)
    var = pm(lambda x: mm_side(x * x, ones_bd) * inv_n, oc)
    bonus = pm(lambda a, b, c: mm_side(a * b * c, ones_bd), r, k, rk)
    y = pm(lambda x, s2, gg, gb, bo, vv, gt: (x * lax.rsqrt(s2 + GN_EPS) * gg + gb + bo * vv) * gt,
           oc, var, gng, gnb, bonus, v, g)
    return y, s_new


def _wkv_kernel(r_ref, lw_ref, k_ref, v_ref, p_ref, q_ref, g_ref, rk_ref, gng_ref, gnb_ref, s0_ref,
                y_ref, so_ref, s_scr, *, n_pairs, n_chunks):
    tb = pl.program_id(2)

    @pl.when(tb == 0)
    def _():
        s_scr[...] = s0_ref[...]

    lane = lax.broadcasted_iota(jnp.int32, (CHUNK, PAIR), 1)
    m0 = lane < HEAD
    c2 = 2 * CHUNK
    ri = lax.broadcasted_iota(jnp.int32, (c2, c2), 0)
    ci = lax.broadcasted_iota(jnp.int32, (c2, c2), 1)
    same = (ri // CHUNK) == (ci // CHUNK)
    bd_strict = same & ((ci % CHUNK) < (ri % CHUNK))
    bd_mask = (ri // HEAD) == (ci // HEAD)
    eye = jnp.where(ri == ci, 1.0, 0.0).astype(F32)
    ones_bd = jnp.where(bd_mask, 1.0, 0.0).astype(BF16)
    ti = lax.broadcasted_iota(jnp.int32, (CHUNK, c2), 0)
    si = lax.broadcasted_iota(jnp.int32, (CHUNK, c2), 1)
    sel_incl = (si % CHUNK) <= ti
    inv_masks = [(ri // 2) == (ci // 2)]
    blk = 2
    while blk < CHUNK:
        inv_masks.append(((ri // (2 * blk)) == (ci // (2 * blk))) & ((ri // blk) != (ci // blk)))
        blk *= 2
    consts = (m0, bd_strict, sel_incl, bd_mask, eye, ones_bd, inv_masks)
    tr = lax.broadcasted_iota(jnp.int32, (CHUNK, CHUNK), 0)
    tc = lax.broadcasted_iota(jnp.int32, (CHUNK, CHUNK), 1)
    tri = jnp.where(tc <= tr, 1.0, 0.0).astype(BF16)

    n_sb = r_ref.shape[0]
    units = [(sb, pr, slice(pr * PAIR, (pr + 1) * PAIR)) for sb in range(n_sb) for pr in range(n_pairs)]

    def chunk_body(c, carry):
        rows = pl.ds(pl.multiple_of(c * CHUNK, CHUNK), CHUNK)
        lw_all, cum_all = [], []
        for sb in range(n_sb):
            lw_sb = lw_ref[sb, rows, :]
            lw_hi, lw_lo = _split(lw_sb)
            lw_all.append(lw_sb)
            cum_all.append(_dot(tri, lw_hi) + _dot(tri, lw_lo))
        seq = lambda ref: [ref[sb, rows, ls] for sb, _, ls in units]
        vec = lambda ref: [ref[:, ls] for _, _, ls in units]
        y, s_new = _wkv_chunk(
            seq(r_ref), [lw_all[sb][:, ls] for sb, _, ls in units], [cum_all[sb][:, ls] for sb, _, ls in units],
            seq(k_ref), seq(v_ref), seq(p_ref), seq(q_ref), seq(g_ref), vec(rk_ref), vec(gng_ref), vec(gnb_ref),
            [s_scr[sb, pr] for sb, pr, _ in units], consts)
        for i, (sb, pr, ls) in enumerate(units):
            s_scr[sb, pr] = s_new[i]
            y_ref[sb, rows, ls] = y[i].astype(y_ref.dtype)
        return carry

    lax.fori_loop(0, n_chunks, chunk_body, 0)

    @pl.when(tb == pl.num_programs(2) - 1)
    def _():
        so_ref[...] = s_scr[...]


def wkv(r, lw, k, v, p, q, g, rk, gng, gnb, s0_bd, t_len):
    rows, d_b = r.shape
    n_streams = rows // t_len
    lanes = WKV_PAIRS * PAIR
    n_pairs = lanes // PAIR
    n_sb = WKV_STREAMS if n_streams % WKV_STREAMS == 0 else 1
    tb_len = _row_tile(t_len, 128)
    seq = pl.BlockSpec((n_sb, tb_len, lanes), lambda s, l, t: (s, t, l))
    vec = pl.BlockSpec((1, lanes), lambda s, l, t: (0, l))
    st = pl.BlockSpec((n_sb, n_pairs, PAIR, PAIR), lambda s, l, t: (s, l, 0, 0))
    per_stream = lambda a: a.reshape(n_streams, t_len, d_b)
    y, s_out = pl.pallas_call(
        functools.partial(_wkv_kernel, n_pairs=n_pairs, n_chunks=tb_len // CHUNK),
        grid=(n_streams // n_sb, d_b // lanes, t_len // tb_len),
        in_specs=[seq] * 7 + [vec] * 3 + [st],
        out_specs=[seq, st],
        out_shape=[
            jax.ShapeDtypeStruct((n_streams, t_len, d_b), BF16),
            jax.ShapeDtypeStruct(s0_bd.shape, F32),
        ],
        scratch_shapes=[pltpu.VMEM((n_sb, n_pairs, PAIR, PAIR), F32)],
        compiler_params=_params(("parallel", "parallel", "arbitrary")),
        name="wkv",
    )(*[per_stream(a) for a in (r, lw, k, v, p, q, g)], rk, gng, gnb, s0_bd)
    return y.reshape(rows, d_b), s_out


def _mix_out_kernel(x_ref, ya_ref, yb_ref, ga_ref, gb_ref, wa_ref, wb_ref, wo_ref, o_ref):
    mixed = (ga_ref[...].astype(F32) * _dot(ya_ref[...], wa_ref[...])
             + gb_ref[...].astype(F32) * _dot(yb_ref[...], wb_ref[...]))
    o_ref[...] = x_ref[...] + _dot(mixed.astype(BF16), wo_ref[...])


def mix_out(x, ya, yb, gates, w_pa, w_pb, w_o):
    rows, d = x.shape
    d_a, d_b = ya.shape[1], yb.shape[1]
    tm = _row_tile(rows, 256)
    resident = lambda a: pl.BlockSpec(a.shape, lambda i: (0, 0), pipeline_mode=pl.Buffered(1))
    return pl.pallas_call(
        _mix_out_kernel,
        grid=(rows // tm,),
        in_specs=[
            pl.BlockSpec((tm, d), lambda i: (i, 0)),
            pl.BlockSpec((tm, d_a), lambda i: (i, 0)),
            pl.BlockSpec((tm, d_b), lambda i: (i, 0)),
            pl.BlockSpec((tm, d), lambda i: (i, 0)),
            pl.BlockSpec((tm, d), lambda i: (i, 1)),
            resident(w_pa), resident(w_pb), resident(w_o),
        ],
        out_specs=pl.BlockSpec((tm, d), lambda i: (i, 0)),
        out_shape=jax.ShapeDtypeStruct((rows, d), F32),
        compiler_params=_params(("parallel",)),
        name="mix_out",
    )(x, ya, yb, gates, gates, w_pa, w_pb, w_o)


def _ffn_kernel(x_ref, g_ref, wu_ref, wd_ref, gf_ref, o_ref, h_ref, acc_ref, *, final_norm):
    f = pl.program_id(1)

    @pl.when(f == 0)
    def _():
        h_ref[...] = _rms(x_ref[...], g_ref[...]).astype(BF16)
        acc_ref[...] = jnp.zeros_like(acc_ref)

    up = jnp.maximum(_dot(h_ref[...], wu_ref[...]), 0.0)
    acc_ref[...] += _dot((up * up).astype(BF16), wd_ref[...])

    @pl.when(f == pl.num_programs(1) - 1)
    def _():
        y = x_ref[...] + acc_ref[...]
        if final_norm:
            y = _rms(y, gf_ref[...])
        o_ref[...] = y


def ffn(x, gain, w_up, w_down, final_gain, final_norm, tf=1024):
    rows, d = x.shape
    d_ff = w_up.shape[1]
    tm = _row_tile(rows, 512)
    return pl.pallas_call(
        functools.partial(_ffn_kernel, final_norm=final_norm),
        grid=(rows // tm, d_ff // tf),
        in_specs=[
            pl.BlockSpec((tm, d), lambda i, f: (i, 0)),
            pl.BlockSpec((1, d), lambda i, f: (0, 0)),
            pl.BlockSpec((d, tf), lambda i, f: (0, f)),
            pl.BlockSpec((tf, d), lambda i, f: (f, 0)),
            pl.BlockSpec((1, d), lambda i, f: (0, 0)),
        ],
        out_specs=pl.BlockSpec((tm, d), lambda i, f: (i, 0)),
        out_shape=jax.ShapeDtypeStruct((rows, d), F32),
        scratch_shapes=[pltpu.VMEM((tm, d), BF16), pltpu.VMEM((tm, d), F32)],
        compiler_params=_params(("parallel", "arbitrary")),
        name="ffn",
    )(x, gain, w_up, w_down, final_gain)


def _to_block_diag(s):
    b, h, n, _ = s.shape
    sr = s.reshape(b, h // 2, 2, n, n)
    z = jnp.zeros_like(sr[:, :, 0])
    top = jnp.concatenate([sr[:, :, 0], z], axis=-1)
    bot = jnp.concatenate([z, sr[:, :, 1]], axis=-1)
    return jnp.concatenate([top, bot], axis=-2)


def _from_block_diag(sbd):
    b, hp, n2, _ = sbd.shape
    n = n2 // 2
    return jnp.stack([sbd[:, :, :n, :n], sbd[:, :, n:, n:]], axis=2).reshape(b, 2 * hp, n, n)


def _layer(x, shift0, s0, t_len, wt, final_gain, final_norm, want_vn):
    rows = x.shape[0]
    n_streams = rows // t_len
    d_a = wt["ln_g"].shape[1]
    d_shift = shift0.shape[1]
    width = wt["mu"].shape[1]

    uv, pj, gates = proj(x, wt["norm1"], wt["w_cat"], 2 * d_a, width)

    blk = MLP_BLOCK if t_len % MLP_BLOCK == 0 else t_len
    ya, *vn = gmlp(uv, wt["ln_g"], wt["ln_b"], wt["w_s"][:, :blk, :blk], wt["b_full"][:, :blk], blk, want_vn)

    shift0p = jnp.pad(shift0, ((0, 0), (0, width - d_shift)))[:, None, :]
    r, lw, k, v, p, q, g = rwkv_prep(pj, shift0p, wt["mu"], wt["w0"], wt["a0"], wt["k_k"], wt["k_a"],
                                     wt["wwa"], wt["g2p"], wt["e1"], wt["e1t"], t_len)
    yb, s_bd = wkv(r, lw, k, v, p, q, g, wt["r_k"], wt["gn_g"], wt["gn_b"], _to_block_diag(s0), t_len)

    x = mix_out(x, ya, yb, gates, wt["w_pa"], wt["w_pb"], wt["w_o"])
    x = ffn(x, wt["norm2"], wt["w_up"], wt["w_down"], final_gain, final_norm)

    new_shift = pj.reshape(n_streams, t_len, width)[:, -1, :d_shift]
    return x, (vn[0] if want_vn else None), new_shift, _from_block_diag(s_bd)


def kernel(x_prompt, x_sample, state_tshift, state_wkv, norm1, w_in, ln_v_g, ln_v_b, w_s, b_s, mu_shift,
           w0, w2, a0, a2, g2, k_k, k_a, r_k, gn_g, gn_b, w_gate, w_pa, w_pb, w_o, norm2, w_up, w_down,
           norm_f):
    depth = w_in.shape[0]
    bp, tp, d = x_prompt.shape
    bs, ts, _ = x_sample.shape
    d_a = ln_v_g.shape[1]
    d_b = w0.shape[1]
    d_shift = mu_shift.shape[1]
    heads = d_b // HEAD
    assert tp % CHUNK == 0 and ts % CHUNK == 0 and heads % 4 == 0
    assert all(t % MLP_BLOCK == 0 or (t < MLP_BLOCK and t % 8 == 0) for t in (tp, ts))
    lo_w, lo_a = w2.shape[1], a2.shape[1]
    assert lo_w == HEAD and lo_a == HEAD
    lo_g = g2.shape[1]
    width = -(-d_shift // PROJ_TN) * PROJ_TN
    g_w = width - (3 * d_b + PAIR)
    dg = d_a // w_s.shape[1]

    head_of = jnp.arange(d_b) // HEAD
    e1 = (head_of[:, None] == jnp.arange(PAIR)[None, :]).astype(BF16)
    row = lambda a: a.reshape(1, -1)

    xp = x_prompt.reshape(bp * tp, d)
    xs = x_sample.reshape(bs * ts, d)
    zero_shift = jnp.zeros((bp, d_shift), F32)
    zero_wkv = jnp.zeros((bp, heads, HEAD, HEAD), F32)
    outs = {n: [] for n in ("tsh_p", "wkv_p", "tsh_s", "wkv_s", "vrow_s")}
    for l in range(depth):
        wwa = jnp.zeros((PAIR, 2 * d_b), F32)
        wwa = wwa.at[:lo_w, :d_b].set(w2[l]).at[lo_w:, d_b:].set(a2[l])
        wt = {
            "norm1": row(norm1[l]), "norm2": row(norm2[l]),
            "w_cat": jnp.concatenate(
                [w_in[l].astype(BF16), jnp.zeros((d, width - d_shift), BF16), w_gate[l].astype(BF16)], axis=1),
            "ln_g": row(ln_v_g[l]), "ln_b": row(ln_v_b[l]),
            "w_s": w_s[l],
            "b_full": jnp.broadcast_to(b_s[l][:, :, None], b_s[l].shape + (dg,)),
            "mu": row(jnp.pad(mu_shift[l], (0, width - d_shift))),
            "w0": row(w0[l]), "a0": row(a0[l]), "k_k": row(k_k[l]), "k_a": row(k_a[l]),
            "wwa": wwa,
            "g2p": jnp.pad(g2[l], ((0, g_w - lo_g), (0, 0))),
            "e1": e1, "e1t": e1.T,
            "r_k": row(r_k[l]), "gn_g": row(gn_g[l]), "gn_b": row(gn_b[l]),
            "w_pa": w_pa[l].astype(BF16), "w_pb": w_pb[l].astype(BF16), "w_o": w_o[l].astype(BF16),
            "w_up": w_up[l].astype(BF16), "w_down": w_down[l].astype(BF16),
        }
        last = l == depth - 1
        gf = row(norm_f)
        xp, _, shp, wkp = _layer(xp, zero_shift, zero_wkv, tp, wt, gf, last, False)
        xs, vrows, shs, wks = _layer(xs, state_tshift[l], state_wkv[l], ts, wt, gf, last, True)
        outs["tsh_p"].append(shp)
        outs["wkv_p"].append(wkp)
        outs["tsh_s"].append(shs)
        outs["wkv_s"].append(wks)
        outs["vrow_s"].append(vrows.reshape(bs, ts, d_a))
    return (xp.reshape(bp, tp, d), xs.reshape(bs, ts, d), jnp.stack(outs["tsh_p"]), jnp.stack(outs["wkv_p"]),
            jnp.stack(outs["tsh_s"]), jnp.stack(outs["wkv_s"]), jnp.stack(outs["vrow_s"]))
```

```python
import functools

import jax
import jax.numpy as jnp
from jax import lax
from jax.experimental import pallas as pl
from jax.experimental.pallas import tpu as pltpu

F32 = jnp.float32
BF16 = jnp.bfloat16

HEAD = 64
PAIR = 2 * HEAD
CHUNK = 64
WKV_PAIRS = 8
WKV_STREAMS = 2
MLP_BLOCK = 128
PROJ_TN = 512
MXU_COLS = 256
RMS_EPS = 1e-5
LN_EPS = 1e-5
GN_EPS = 64e-5
DECAY_SCALE = 0.6065306597126334
VMEM_LIMIT = 56 * 1024 * 1024

NN = (((1,), (0,)), ((), ()))
NT = (((1,), (1,)), ((), ()))


def _params(sem):
    return pltpu.CompilerParams(dimension_semantics=sem, vmem_limit_bytes=VMEM_LIMIT)


def _dot(a, b, dims=NN):
    return lax.dot_general(a, b, dims, preferred_element_type=F32)


def _split(x):
    hi = x.astype(BF16)
    lo = (x - hi.astype(F32)).astype(BF16)
    return hi, lo


def _mm3(a, b, dims=NN):
    ah, al = _split(a)
    bh, bl = _split(b)
    return _dot(ah, bh, dims) + (_dot(ah, bl, dims) + _dot(al, bh, dims))


def _mm1(a, b, dims=NN):
    return _dot(a.astype(BF16), b.astype(BF16), dims)


def _mm2(a, b_exact):
    ah, al = _split(a)
    return _dot(ah, b_exact) + _dot(al, b_exact)


def _rms(x, g):
    return x * lax.rsqrt(jnp.mean(x * x, axis=-1, keepdims=True) + RMS_EPS) * g


def _gelu(x):
    return 0.5 * x * (1.0 + jnp.tanh(0.7978845608028654 * (x + 0.044715 * (x * x * x))))


def _sigmoid(x):
    return 1.0 / (1.0 + jnp.exp(-x))


def _identity(x):
    return x


def _row_tile(rows, cap):
    t = min(rows, cap)
    while rows % t:
        t //= 2
    return t


def _proj_kernel(x_ref, g_ref, w_ref, uv_ref, pj_ref, gt_ref, h_ref, *, n_uv, n_pj):
    j = pl.program_id(1)

    @pl.when(j == 0)
    def _():
        h_ref[...] = _rms(x_ref[...], g_ref[...]).astype(BF16)

    def emit(o_ref, act):
        for c in range(0, o_ref.shape[1], MXU_COLS):
            o_ref[:, c:c + MXU_COLS] = act(_dot(h_ref[...], w_ref[:, c:c + MXU_COLS])).astype(o_ref.dtype)

    @pl.when(j < n_uv)
    def _():
        emit(uv_ref, _gelu)

    @pl.when((j >= n_uv) & (j < n_uv + n_pj))
    def _():
        emit(pj_ref, _identity)

    @pl.when(j >= n_uv + n_pj)
    def _():
        emit(gt_ref, _identity)


def proj(x, gain, w_cat, n_uv_cols, n_pj_cols):
    rows, d = x.shape
    n = w_cat.shape[1]
    tn = PROJ_TN
    n_uv, n_pj = n_uv_cols // tn, n_pj_cols // tn
    n_gt = n // tn - n_uv - n_pj
    tm = _row_tile(rows, 1024)
    return pl.pallas_call(
        functools.partial(_proj_kernel, n_uv=n_uv, n_pj=n_pj),
        grid=(rows // tm, n // tn),
        in_specs=[
            pl.BlockSpec((tm, d), lambda i, j: (i, 0)),
            pl.BlockSpec((1, d), lambda i, j: (0, 0)),
            pl.BlockSpec((d, tn), lambda i, j: (0, j)),
        ],
        out_specs=[
            pl.BlockSpec((tm, tn), lambda i, j: (i, jnp.minimum(j, n_uv - 1))),
            pl.BlockSpec((tm, tn), lambda i, j: (i, jnp.clip(j - n_uv, 0, n_pj - 1))),
            pl.BlockSpec((tm, tn), lambda i, j: (i, jnp.clip(j - n_uv - n_pj, 0, n_gt - 1))),
        ],
        out_shape=[
            jax.ShapeDtypeStruct((rows, n_uv * tn), F32),
            jax.ShapeDtypeStruct((rows, n_pj * tn), F32),
            jax.ShapeDtypeStruct((rows, n_gt * tn), BF16),
        ],
        scratch_shapes=[pltpu.VMEM((tm, d), BF16)],
        compiler_params=_params(("parallel", "arbitrary")),
        name="proj",
    )(x, gain, w_cat)


def _gmlp_kernel(u_ref, v_ref, lg_ref, lb_ref, ws_ref, bs_ref, ya_ref, *maybe_vn_ref, blk, groups):
    v = v_ref[...]
    mu = jnp.mean(v, axis=-1, keepdims=True)
    vc = v - mu
    var = jnp.mean(vc * vc, axis=-1, keepdims=True)
    vn = vc * lax.rsqrt(var + LN_EPS) * lg_ref[...] + lb_ref[...]
    for vn_ref in maybe_vn_ref:
        vn_ref[...] = vn
    vnb = vn.astype(BF16)
    q_idx = lax.broadcasted_iota(jnp.int32, (blk, blk), 0)
    k_idx = lax.broadcasted_iota(jnp.int32, (blk, blk), 1)
    causal = k_idx <= q_idx
    dg = v.shape[1] // groups
    for g in range(groups):
        wg = jnp.where(causal, ws_ref[g], 0.0).astype(BF16)
        bg = bs_ref[g]
        for c in range(v.shape[0] // blk):
            rs = slice(c * blk, (c + 1) * blk)
            cs = slice(g * dg, (g + 1) * dg)
            s = _dot(wg, vnb[rs, cs]) + bg
            ya_ref[rs, cs] = (u_ref[rs, cs] * s).astype(ya_ref.dtype)


def gmlp(uv, ln_g, ln_b, ws, bs_full, blk, want_vn):
    rows = uv.shape[0]
    d_a = uv.shape[1] // 2
    groups = ws.shape[0]
    tm = _row_tile(rows, max(blk, 256))
    n_out = 2 if want_vn else 1
    return pl.pallas_call(
        functools.partial(_gmlp_kernel, blk=blk, groups=groups),
        grid=(rows // tm,),
        in_specs=[
            pl.BlockSpec((tm, d_a), lambda i: (i, 0)),
            pl.BlockSpec((tm, d_a), lambda i: (i, 1)),
            pl.BlockSpec((1, d_a), lambda i: (0, 0)),
            pl.BlockSpec((1, d_a), lambda i: (0, 0)),
            pl.BlockSpec((groups, blk, blk), lambda i: (0, 0, 0)),
            pl.BlockSpec((groups, blk, d_a // groups), lambda i: (0, 0, 0)),
        ],
        out_specs=[pl.BlockSpec((tm, d_a), lambda i: (i, 0))] * n_out,
        out_shape=[
            jax.ShapeDtypeStruct((rows, d_a), BF16),
            jax.ShapeDtypeStruct((rows, d_a), F32),
        ][:n_out],
        compiler_params=_params(("parallel",)),
        name="gmlp",
    )(uv, uv, ln_g, ln_b, ws, bs_full)


def _prep_kernel(p_ref, halo_ref, s0_ref, mu_ref, w0_ref, a0_ref, kk_ref, ka_ref, wwa_ref, g2_ref,
                 e1_ref, e1t_ref, r_o, lw_o, k_o, v_o, p_o, q_o, g_o, *, blocks_per_stream, d_b):
    i = pl.program_id(0)
    pb = p_ref[...]
    first = jnp.where(i % blocks_per_stream == 0, s0_ref[...], halo_ref[7:8, :])
    row = lax.broadcasted_iota(jnp.int32, pb.shape, 0)
    prev = jnp.where(row == 0, first, pltpu.roll(pb, 1, 0))
    xs = pb + (prev - pb) * mu_ref[...]

    r = xs[:, 0:d_b]
    k = xs[:, d_b:2 * d_b]
    v = xs[:, 2 * d_b:3 * d_b]
    wa = xs[:, 3 * d_b:3 * d_b + PAIR]
    gl = xs[:, 3 * d_b + PAIR:]
    lane = lax.broadcasted_iota(jnp.int32, wa.shape, 1)
    lora = _mm3(jnp.where(lane < HEAD, jnp.tanh(wa), wa), wwa_ref[...])
    lw = -DECAY_SCALE * _sigmoid(w0_ref[...] + lora[:, :d_b])
    a = _sigmoid(a0_ref[...] + lora[:, d_b:])
    g = _mm3(_sigmoid(gl), g2_ref[...])

    kk = k * kk_ref[...]
    ss = _mm2(_mm2(kk * kk, e1_ref[...]), e1t_ref[...])
    kk = kk * lax.rsqrt(jnp.maximum(ss, 1e-24))

    r_o[...] = r
    lw_o[...] = lw
    k_o[...] = k * (1.0 + (a - 1.0) * ka_ref[...])
    v_o[...] = v
    p_o[...] = kk
    q_o[...] = -(kk * a)
    g_o[...] = g


def rwkv_prep(p, shift0, mu, w0, a0, k_k, k_a, wwa, g2p, e1, e1t, t_len):
    rows, width = p.shape
    d_b = w0.shape[1]
    tm = _row_tile(t_len, 256)
    bps = t_len // tm
    vec = lambda n: pl.BlockSpec((1, n), lambda i: (0, 0))
    full = lambda a: pl.BlockSpec(a.shape, lambda i: (0, 0))
    out = pl.BlockSpec((tm, d_b), lambda i: (i, 0))
    return pl.pallas_call(
        functools.partial(_prep_kernel, blocks_per_stream=bps, d_b=d_b),
        grid=(rows // tm,),
        in_specs=[
            pl.BlockSpec((tm, width), lambda i: (i, 0)),
            pl.BlockSpec((8, width), lambda i: (jnp.maximum(i * (tm // 8) - 1, 0), 0)),
            pl.BlockSpec((None, 1, width), lambda i: (i // bps, 0, 0)),
            vec(width), vec(d_b), vec(d_b), vec(d_b), vec(d_b),
            full(wwa), full(g2p), full(e1), full(e1t),
        ],
        out_specs=[out] * 7,
        out_shape=[jax.ShapeDtypeStruct((rows, d_b), F32)] * 7,
        compiler_params=_params(("parallel",)),
        name="rwkv_prep",
    )(p, p, shift0, mu, w0, a0, k_k, k_a, wwa, g2p, e1, e1t)


def _wkv_chunk(r, lw, cum, k, v, p, q, g, rk, gng, gnb, S, consts):
    m0, bd_strict, sel_incl, bd_mask, eye, ones_bd, inv_masks = consts
    c2 = 2 * CHUNK
    pm = lambda f, *ls: [f(*a) for a in zip(*ls)]
    mm_state, mm_side = _mm1, _mm1
    zero = jnp.zeros((CHUNK, PAIR), F32)
    stack = lambda x: jnp.concatenate([jnp.where(m0, x, zero), jnp.where(m0, zero, x)], axis=0)
    pick = lambda blk: jnp.where(m0, blk[:CHUNK], blk[CHUNK:])

    cum_end = pm(lambda c: c[CHUNK - 1:CHUNK, :], cum)
    g_in = pm(jnp.exp, cum)
    g_ex = pm(lambda c, l: jnp.exp(c - l), cum, lw)
    g_inv = pm(lambda c: jnp.exp(-c), cum)
    g_bar = pm(lambda ce, c: jnp.exp(ce - c), cum_end, cum)
    mul = lambda a, b: a * b
    pt, rt = pm(mul, p, g_ex), pm(mul, r, g_in)
    qt, kt = pm(mul, q, g_inv), pm(mul, k, g_inv)
    qb, kb = pm(mul, q, g_bar), pm(mul, k, g_bar)

    a_all = pm(lambda a, b, c, d: mm_side(jnp.concatenate([stack(a), stack(b)], axis=0),
                                          jnp.concatenate([c, c, d, d], axis=0), NT), pt, rt, qt, kt)
    x_all = pm(lambda a, b, s: mm_state(jnp.concatenate([a, b], axis=0), s, NT), pt, rt, S)
    a_pq = pm(lambda a: jnp.where(bd_strict, a[:c2, :c2], 0.0), a_all)
    a_pk = pm(lambda a: jnp.where(bd_strict, a[:c2, c2:], 0.0), a_all)
    a_rq = pm(lambda a: jnp.where(sel_incl, pick(a[c2:, :c2]), 0.0), a_all)
    a_rk = pm(lambda a: jnp.where(sel_incl, pick(a[c2:, c2:]), 0.0), a_all)
    v_st = pm(stack, v)
    w0 = pm(mm_side, a_pk, v_st)

    t_acc = pm(lambda a: eye + jnp.where(inv_masks[0], a, 0.0), a_pq)
    for off in inv_masks[1:]:
        y = pm(lambda a, t: mm_side(jnp.where(off, a, 0.0), t), a_pq, t_acc)
        t_acc = pm(lambda t, yy: t + mm_side(t, yy), t_acc, y)

    u_st = pm(lambda t, x, w: mm_state(t, stack(x[:CHUNK]) + w), t_acc, x_all, w0)
    o = pm(lambda x, aq, ak, u, vs: x[CHUNK:] + mm_side(jnp.concatenate([aq, ak], axis=1),
                                                         jnp.concatenate([u, vs], axis=0)),
           x_all, a_rq, a_rk, u_st, v_st)
    upd = pm(lambda u, vv, a, b: mm_state(jnp.concatenate([u[:CHUNK] + u[CHUNK:], vv], axis=0).T,
                                          jnp.concatenate([a, b], axis=0)), u_st, v, qb, kb)
    s_new = pm(lambda s, ce, u: s * jnp.exp(ce) + jnp.where(bd_mask, u, 0.0), S, cum_end, upd)

    inv_n = 1.0 / HEAD
    mean = pm(lambda x: mm_side(x, ones_bd) * inv_n, o)
    oc = pm(lambda x, m: x - m, o, mean)
    var = pm(lambda x: mm_side(x * x, ones_bd) * inv_n, oc)
    bonus = pm(lambda a, b, c: mm_side(a * b * c, ones_bd), r, k, rk)
    y = pm(lambda x, s2, gg, gb, bo, vv, gt: (x * lax.rsqrt(s2 + GN_EPS) * gg + gb + bo * vv) * gt,
           oc, var, gng, gnb, bonus, v, g)
    return y, s_new


def _wkv_kernel(r_ref, lw_ref, k_ref, v_ref, p_ref, q_ref, g_ref, rk_ref, gng_ref, gnb_ref, s0_ref,
                y_ref, so_ref, s_scr, *, n_pairs, n_chunks):
    tb = pl.program_id(2)

    @pl.when(tb == 0)
    def _():
        s_scr[...] = s0_ref[...]

    lane = lax.broadcasted_iota(jnp.int32, (CHUNK, PAIR), 1)
    m0 = lane < HEAD
    c2 = 2 * CHUNK
    ri = lax.broadcasted_iota(jnp.int32, (c2, c2), 0)
    ci = lax.broadcasted_iota(jnp.int32, (c2, c2), 1)
    same = (ri // CHUNK) == (ci // CHUNK)
    bd_strict = same & ((ci % CHUNK) < (ri % CHUNK))
    bd_mask = (ri // HEAD) == (ci // HEAD)
    eye = jnp.where(ri == ci, 1.0, 0.0).astype(F32)
    ones_bd = jnp.where(bd_mask, 1.0, 0.0).astype(BF16)
    ti = lax.broadcasted_iota(jnp.int32, (CHUNK, c2), 0)
    si = lax.broadcasted_iota(jnp.int32, (CHUNK, c2), 1)
    sel_incl = (si % CHUNK) <= ti
    inv_masks = [(ri // 2) == (ci // 2)]
    blk = 2
    while blk < CHUNK:
        inv_masks.append(((ri // (2 * blk)) == (ci // (2 * blk))) & ((ri // blk) != (ci // blk)))
        blk *= 2
    consts = (m0, bd_strict, sel_incl, bd_mask, eye, ones_bd, inv_masks)
    tr = lax.broadcasted_iota(jnp.int32, (CHUNK, CHUNK), 0)
    tc = lax.broadcasted_iota(jnp.int32, (CHUNK, CHUNK), 1)
    tri = jnp.where(tc <= tr, 1.0, 0.0).astype(BF16)

    n_sb = r_ref.shape[0]
    units = [(sb, pr, slice(pr * PAIR, (pr + 1) * PAIR)) for sb in range(n_sb) for pr in range(n_pairs)]

    def chunk_body(c, carry):
        rows = pl.ds(pl.multiple_of(c * CHUNK, CHUNK), CHUNK)
        lw_all, cum_all = [], []
        for sb in range(n_sb):
            lw_sb = lw_ref[sb, rows, :]
            lw_hi, lw_lo = _split(lw_sb)
            lw_all.append(lw_sb)
            cum_all.append(_dot(tri, lw_hi) + _dot(tri, lw_lo))
        seq = lambda ref: [ref[sb, rows, ls] for sb, _, ls in units]
        vec = lambda ref: [ref[:, ls] for _, _, ls in units]
        y, s_new = _wkv_chunk(
            seq(r_ref), [lw_all[sb][:, ls] for sb, _, ls in units], [cum_all[sb][:, ls] for sb, _, ls in units],
            seq(k_ref), seq(v_ref), seq(p_ref), seq(q_ref), seq(g_ref), vec(rk_ref), vec(gng_ref), vec(gnb_ref),
            [s_scr[sb, pr] for sb, pr, _ in units], consts)
        for i, (sb, pr, ls) in enumerate(units):
            s_scr[sb, pr] = s_new[i]
            y_ref[sb, rows, ls] = y[i].astype(y_ref.dtype)
        return carry

    lax.fori_loop(0, n_chunks, chunk_body, 0)

    @pl.when(tb == pl.num_programs(2) - 1)
    def _():
        so_ref[...] = s_scr[...]


def wkv(r, lw, k, v, p, q, g, rk, gng, gnb, s0_bd, t_len):
    rows, d_b = r.shape
    n_streams = rows // t_len
    lanes = WKV_PAIRS * PAIR
    n_pairs = lanes // PAIR
    n_sb = WKV_STREAMS if n_streams % WKV_STREAMS == 0 else 1
    tb_len = _row_tile(t_len, 128)
    seq = pl.BlockSpec((n_sb, tb_len, lanes), lambda s, l, t: (s, t, l))
    vec = pl.BlockSpec((1, lanes), lambda s, l, t: (0, l))
    st = pl.BlockSpec((n_sb, n_pairs, PAIR, PAIR), lambda s, l, t: (s, l, 0, 0))
    per_stream = lambda a: a.reshape(n_streams, t_len, d_b)
    y, s_out = pl.pallas_call(
        functools.partial(_wkv_kernel, n_pairs=n_pairs, n_chunks=tb_len // CHUNK),
        grid=(n_streams // n_sb, d_b // lanes, t_len // tb_len),
        in_specs=[seq] * 7 + [vec] * 3 + [st],
        out_specs=[seq, st],
        out_shape=[
            jax.ShapeDtypeStruct((n_streams, t_len, d_b), BF16),
            jax.ShapeDtypeStruct(s0_bd.shape, F32),
        ],
        scratch_shapes=[pltpu.VMEM((n_sb, n_pairs, PAIR, PAIR), F32)],
        compiler_params=_params(("parallel", "parallel", "arbitrary")),
        name="wkv",
    )(*[per_stream(a) for a in (r, lw, k, v, p, q, g)], rk, gng, gnb, s0_bd)
    return y.reshape(rows, d_b), s_out


def _mix_out_kernel(x_ref, ya_ref, yb_ref, ga_ref, gb_ref, wa_ref, wb_ref, wo_ref, o_ref):
    gate = lambda ref: 0.5 + 0.5 * jnp.tanh(0.5 * ref[...].astype(F32))
    mixed = gate(ga_ref) * _dot(ya_ref[...], wa_ref[...]) + gate(gb_ref) * _dot(yb_ref[...], wb_ref[...])
    o_ref[...] = x_ref[...] + _dot(mixed.astype(BF16), wo_ref[...])


def mix_out(x, ya, yb, gates, w_pa, w_pb, w_o):
    rows, d = x.shape
    d_a, d_b = ya.shape[1], yb.shape[1]
    tm = _row_tile(rows, 256)
    resident = lambda a: pl.BlockSpec(a.shape, lambda i: (0, 0), pipeline_mode=pl.Buffered(1))
    return pl.pallas_call(
        _mix_out_kernel,
        grid=(rows // tm,),
        in_specs=[
            pl.BlockSpec((tm, d), lambda i: (i, 0)),
            pl.BlockSpec((tm, d_a), lambda i: (i, 0)),
            pl.BlockSpec((tm, d_b), lambda i: (i, 0)),
            pl.BlockSpec((tm, d), lambda i: (i, 0)),
            pl.BlockSpec((tm, d), lambda i: (i, 1)),
            resident(w_pa), resident(w_pb), resident(w_o),
        ],
        out_specs=pl.BlockSpec((tm, d), lambda i: (i, 0)),
        out_shape=jax.ShapeDtypeStruct((rows, d), F32),
        compiler_params=_params(("parallel",)),
        name="mix_out",
    )(x, ya, yb, gates, gates, w_pa, w_pb, w_o)


def _ffn_kernel(x_ref, g_ref, wu_ref, wd_ref, gf_ref, o_ref, h_ref, acc_ref, *, final_norm):
    f = pl.program_id(1)

    @pl.when(f == 0)
    def _():
        h_ref[...] = _rms(x_ref[...], g_ref[...]).astype(BF16)
        acc_ref[...] = jnp.zeros_like(acc_ref)

    up = jnp.maximum(_dot(h_ref[...], wu_ref[...]), 0.0)
    acc_ref[...] += _dot((up * up).astype(BF16), wd_ref[...])

    @pl.when(f == pl.num_programs(1) - 1)
    def _():
        y = x_ref[...] + acc_ref[...]
        if final_norm:
            y = _rms(y, gf_ref[...])
        o_ref[...] = y


def ffn(x, gain, w_up, w_down, final_gain, final_norm, tf=1024):
    rows, d = x.shape
    d_ff = w_up.shape[1]
    tm = _row_tile(rows, 512)
    return pl.pallas_call(
        functools.partial(_ffn_kernel, final_norm=final_norm),
        grid=(rows // tm, d_ff // tf),
        in_specs=[
            pl.BlockSpec((tm, d), lambda i, f: (i, 0)),
            pl.BlockSpec((1, d), lambda i, f: (0, 0)),
            pl.BlockSpec((d, tf), lambda i, f: (0, f)),
            pl.BlockSpec((tf, d), lambda i, f: (f, 0)),
            pl.BlockSpec((1, d), lambda i, f: (0, 0)),
        ],
        out_specs=pl.BlockSpec((tm, d), lambda i, f: (i, 0)),
        out_shape=jax.ShapeDtypeStruct((rows, d), F32),
        scratch_shapes=[pltpu.VMEM((tm, d), BF16), pltpu.VMEM((tm, d), F32)],
        compiler_params=_params(("parallel", "arbitrary")),
        name="ffn",
    )(x, gain, w_up, w_down, final_gain)


def _to_block_diag(s):
    b, h, n, _ = s.shape
    sr = s.reshape(b, h // 2, 2, n, n)
    z = jnp.zeros_like(sr[:, :, 0])
    top = jnp.concatenate([sr[:, :, 0], z], axis=-1)
    bot = jnp.concatenate([z, sr[:, :, 1]], axis=-1)
    return jnp.concatenate([top, bot], axis=-2)


def _from_block_diag(sbd):
    b, hp, n2, _ = sbd.shape
    n = n2 // 2
    return jnp.stack([sbd[:, :, :n, :n], sbd[:, :, n:, n:]], axis=2).reshape(b, 2 * hp, n, n)


def _layer(x, shift0, s0, t_len, wt, final_gain, final_norm, want_vn):
    rows = x.shape[0]
    n_streams = rows // t_len
    d_a = wt["ln_g"].shape[1]
    d_shift = shift0.shape[1]
    width = wt["mu"].shape[1]

    uv, pj, gates = proj(x, wt["norm1"], wt["w_cat"], 2 * d_a, width)

    blk = MLP_BLOCK if t_len % MLP_BLOCK == 0 else t_len
    ya, *vn = gmlp(uv, wt["ln_g"], wt["ln_b"], wt["w_s"][:, :blk, :blk], wt["b_full"][:, :blk], blk, want_vn)

    shift0p = jnp.pad(shift0, ((0, 0), (0, width - d_shift)))[:, None, :]
    r, lw, k, v, p, q, g = rwkv_prep(pj, shift0p, wt["mu"], wt["w0"], wt["a0"], wt["k_k"], wt["k_a"],
                                     wt["wwa"], wt["g2p"], wt["e1"], wt["e1t"], t_len)
    yb, s_bd = wkv(r, lw, k, v, p, q, g, wt["r_k"], wt["gn_g"], wt["gn_b"], _to_block_diag(s0), t_len)

    x = mix_out(x, ya, yb, gates, wt["w_pa"], wt["w_pb"], wt["w_o"])
    x = ffn(x, wt["norm2"], wt["w_up"], wt["w_down"], final_gain, final_norm)

    new_shift = pj.reshape(n_streams, t_len, width)[:, -1, :d_shift]
    return x, (vn[0] if want_vn else None), new_shift, _from_block_diag(s_bd)


def kernel(x_prompt, x_sample, state_tshift, state_wkv, norm1, w_in, ln_v_g, ln_v_b, w_s, b_s, mu_shift,
           w0, w2, a0, a2, g2, k_k, k_a, r_k, gn_g, gn_b, w_gate, w_pa, w_pb, w_o, norm2, w_up, w_down,
           norm_f):
    depth = w_in.shape[0]
    bp, tp, d = x_prompt.shape
    bs, ts, _ = x_sample.shape
    d_a = ln_v_g.shape[1]
    d_b = w0.shape[1]
    d_shift = mu_shift.shape[1]
    heads = d_b // HEAD
    assert tp % CHUNK == 0 and ts % CHUNK == 0 and heads % 4 == 0
    assert all(t % MLP_BLOCK == 0 or (t < MLP_BLOCK and t % 8 == 0) for t in (tp, ts))
    lo_w, lo_a = w2.shape[1], a2.shape[1]
    assert lo_w == HEAD and lo_a == HEAD
    lo_g = g2.shape[1]
    width = -(-d_shift // PROJ_TN) * PROJ_TN
    g_w = width - (3 * d_b + PAIR)
    dg = d_a // w_s.shape[1]

    head_of = jnp.arange(d_b) // HEAD
    e1 = (head_of[:, None] == jnp.arange(PAIR)[None, :]).astype(BF16)
    row = lambda a: a.reshape(1, -1)

    xp = x_prompt.reshape(bp * tp, d)
    xs = x_sample.reshape(bs * ts, d)
    zero_shift = jnp.zeros((bp, d_shift), F32)
    zero_wkv = jnp.zeros((bp, heads, HEAD, HEAD), F32)
    outs = {n: [] for n in ("tsh_p", "wkv_p", "tsh_s", "wkv_s", "vrow_s")}
    for l in range(depth):
        wwa = jnp.zeros((PAIR, 2 * d_b), F32)
        wwa = wwa.at[:lo_w, :d_b].set(w2[l]).at[lo_w:, d_b:].set(a2[l])
        wt = {
            "norm1": row(norm1[l]), "norm2": row(norm2[l]),
            "w_cat": jnp.concatenate(
                [w_in[l].astype(BF16), jnp.zeros((d, width - d_shift), BF16), w_gate[l].astype(BF16)], axis=1),
            "ln_g": row(ln_v_g[l]), "ln_b": row(ln_v_b[l]),
            "w_s": w_s[l],
            "b_full": jnp.broadcast_to(b_s[l][:, :, None], b_s[l].shape + (dg,)),
            "mu": row(jnp.pad(mu_shift[l], (0, width - d_shift))),
            "w0": row(w0[l]), "a0": row(a0[l]), "k_k": row(k_k[l]), "k_a": row(k_a[l]),
            "wwa": wwa,
            "g2p": jnp.pad(g2[l], ((0, g_w - lo_g), (0, 0))),
            "e1": e1, "e1t": e1.T,
            "r_k": row(r_k[l]), "gn_g": row(gn_g[l]), "gn_b": row(gn_b[l]),
            "w_pa": w_pa[l].astype(BF16), "w_pb": w_pb[l].astype(BF16), "w_o": w_o[l].astype(BF16),
            "w_up": w_up[l].astype(BF16), "w_down": w_down[l].astype(BF16),
        }
        last = l == depth - 1
        gf = row(norm_f)
        xp, _, shp, wkp = _layer(xp, zero_shift, zero_wkv, tp, wt, gf, last, False)
        xs, vrows, shs, wks = _layer(xs, state_tshift[l], state_wkv[l], ts, wt, gf, last, True)
        outs["tsh_p"].append(shp)
        outs["wkv_p"].append(wkp)
        outs["tsh_s"].append(shs)
        outs["wkv_s"].append(wks)
        outs["vrow_s"].append(vrows.reshape(bs, ts, d_a))
    return (xp.reshape(bp, tp, d), xs.reshape(bs, ts, d), jnp.stack(outs["tsh_p"]), jnp.stack(outs["wkv_p"]),
            jnp.stack(outs["tsh_s"]), jnp.stack(outs["wkv_s"]), jnp.stack(outs["vrow_s"]))
```

```python
import functools

import jax
import jax.numpy as jnp
from jax import lax
from jax.experimental import pallas as pl
from jax.experimental.pallas import tpu as pltpu

F32 = jnp.float32
BF16 = jnp.bfloat16

HEAD = 64
PAIR = 2 * HEAD
CHUNK = 64
WKV_STREAMS = 2
MLP_BLOCK = 128
PROJ_TN = 512
MXU_COLS = 256
RMS_EPS = 1e-5
LN_EPS = 1e-5
GN_EPS = 64e-5
DECAY_SCALE = 0.6065306597126334
VMEM_LIMIT = 56 * 1024 * 1024

NN = (((1,), (0,)), ((), ()))
NT = (((1,), (1,)), ((), ()))


def _params(sem):
    return pltpu.CompilerParams(dimension_semantics=sem, vmem_limit_bytes=VMEM_LIMIT)


def _dot(a, b, dims=NN):
    return lax.dot_general(a, b, dims, preferred_element_type=F32)


def _split(x):
    hi = x.astype(BF16)
    lo = (x - hi.astype(F32)).astype(BF16)
    return hi, lo


def _mm1(a, b, dims=NN):
    return _dot(a.astype(BF16), b.astype(BF16), dims)


def _mm2(a, b_exact):
    ah, al = _split(a)
    return _dot(ah, b_exact) + _dot(al, b_exact)


def _rms(x, g):
    return x * lax.rsqrt(jnp.mean(x * x, axis=-1, keepdims=True) + RMS_EPS) * g


def _gelu(x):
    return 0.5 * x * (1.0 + jnp.tanh(0.7978845608028654 * (x + 0.044715 * (x * x * x))))


def _sigmoid(x):
    return 1.0 / (1.0 + jnp.exp(-x))


def _identity(x):
    return x


def _row_tile(rows, cap):
    t = min(rows, cap)
    while rows % t:
        t //= 2
    return t


def _proj_kernel(x_ref, g_ref, w_ref, uv_ref, pj_ref, gt_ref, h_ref, *, n_uv, n_pj):
    j = pl.program_id(1)

    @pl.when(j == 0)
    def _():
        h_ref[...] = _rms(x_ref[...], g_ref[...]).astype(BF16)

    def emit(o_ref, act):
        for c in range(0, o_ref.shape[1], MXU_COLS):
            o_ref[:, c:c + MXU_COLS] = act(_dot(h_ref[...], w_ref[:, c:c + MXU_COLS])).astype(o_ref.dtype)

    @pl.when(j < n_uv)
    def _():
        emit(uv_ref, _gelu)

    @pl.when((j >= n_uv) & (j < n_uv + n_pj))
    def _():
        emit(pj_ref, _identity)

    @pl.when(j >= n_uv + n_pj)
    def _():
        emit(gt_ref, _identity)


def proj(x, gain, w_cat, layer, n_uv_cols, n_pj_cols):
    rows, d = x.shape
    n = w_cat.shape[2]
    tn = PROJ_TN
    n_uv, n_pj = n_uv_cols // tn, n_pj_cols // tn
    n_gt = n // tn - n_uv - n_pj
    tm = _row_tile(rows, 1024)
    return pl.pallas_call(
        functools.partial(_proj_kernel, n_uv=n_uv, n_pj=n_pj),
        grid=(rows // tm, n // tn),
        in_specs=[
            pl.BlockSpec((tm, d), lambda i, j: (i, 0)),
            pl.BlockSpec((1, d), lambda i, j: (0, 0)),
            pl.BlockSpec((None, d, tn), lambda i, j: (layer, 0, j)),
        ],
        out_specs=[
            pl.BlockSpec((tm, tn), lambda i, j: (i, jnp.minimum(j, n_uv - 1))),
            pl.BlockSpec((tm, tn), lambda i, j: (i, jnp.clip(j - n_uv, 0, n_pj - 1))),
            pl.BlockSpec((tm, tn), lambda i, j: (i, jnp.clip(j - n_uv - n_pj, 0, n_gt - 1))),
        ],
        out_shape=[
            jax.ShapeDtypeStruct((rows, n_uv * tn), F32),
            jax.ShapeDtypeStruct((rows, n_pj * tn), F32),
            jax.ShapeDtypeStruct((rows, n_gt * tn), BF16),
        ],
        scratch_shapes=[pltpu.VMEM((tm, d), BF16)],
        compiler_params=_params(("parallel", "arbitrary")),
        name="proj",
    )(x, gain, w_cat)


def _gmlp_kernel(u_ref, v_ref, lg_ref, lb_ref, ws_ref, bs_ref, ya_ref, *maybe_vn_ref, blk, groups):
    v = v_ref[...]
    mu = jnp.mean(v, axis=-1, keepdims=True)
    vc = v - mu
    var = jnp.mean(vc * vc, axis=-1, keepdims=True)
    vn = vc * lax.rsqrt(var + LN_EPS) * lg_ref[...] + lb_ref[...]
    for vn_ref in maybe_vn_ref:
        vn_ref[...] = vn
    vnb = vn.astype(BF16)
    q_idx = lax.broadcasted_iota(jnp.int32, (blk, blk), 0)
    k_idx = lax.broadcasted_iota(jnp.int32, (blk, blk), 1)
    causal = k_idx <= q_idx
    dg = v.shape[1] // groups
    for g in range(groups):
        wg = jnp.where(causal, ws_ref[g], 0.0).astype(BF16)
        bg = bs_ref[g]
        for c in range(v.shape[0] // blk):
            rs = slice(c * blk, (c + 1) * blk)
            cs = slice(g * dg, (g + 1) * dg)
            s = _dot(wg, vnb[rs, cs]) + bg
            ya_ref[rs, cs] = (u_ref[rs, cs] * s).astype(ya_ref.dtype)


def gmlp(uv, ln_g, ln_b, ws, bs_full, blk, want_vn):
    rows = uv.shape[0]
    d_a = uv.shape[1] // 2
    groups = ws.shape[0]
    tm = _row_tile(rows, max(blk, 256))
    n_out = 2 if want_vn else 1
    return pl.pallas_call(
        functools.partial(_gmlp_kernel, blk=blk, groups=groups),
        grid=(rows // tm,),
        in_specs=[
            pl.BlockSpec((tm, d_a), lambda i: (i, 0)),
            pl.BlockSpec((tm, d_a), lambda i: (i, 1)),
            pl.BlockSpec((1, d_a), lambda i: (0, 0)),
            pl.BlockSpec((1, d_a), lambda i: (0, 0)),
            pl.BlockSpec((groups, blk, blk), lambda i: (0, 0, 0)),
            pl.BlockSpec((groups, blk, d_a // groups), lambda i: (0, 0, 0)),
        ],
        out_specs=[pl.BlockSpec((tm, d_a), lambda i: (i, 0))] * n_out,
        out_shape=[
            jax.ShapeDtypeStruct((rows, d_a), BF16),
            jax.ShapeDtypeStruct((rows, d_a), F32),
        ][:n_out],
        compiler_params=_params(("parallel",)),
        name="gmlp",
    )(uv, uv, ln_g, ln_b, ws, bs_full)


def _rwkv_inputs(pb, prev_row, mu, w0, a0, k_k, k_a, wwa, g2, ones_bd, d_b):
    row = lax.broadcasted_iota(jnp.int32, pb.shape, 0)
    prev = jnp.where(row == 0, prev_row, pltpu.roll(pb, 1, 0))
    xs = pb + (prev - pb) * mu
    r = xs[:, 0:d_b]
    k = xs[:, d_b:2 * d_b]
    v = xs[:, 2 * d_b:3 * d_b]
    wa = xs[:, 3 * d_b:3 * d_b + PAIR]
    gl = xs[:, 3 * d_b + PAIR:]
    lane = lax.broadcasted_iota(jnp.int32, wa.shape, 1)
    lora = _mm1(jnp.where(lane < HEAD, jnp.tanh(wa), wa), wwa)
    lw = -DECAY_SCALE * _sigmoid(w0 + lora[:, :d_b])
    a = _sigmoid(a0 + lora[:, d_b:])
    g = _mm1(_sigmoid(gl), g2)
    kk = k * k_k
    sq = kk * kk
    ss = jnp.concatenate([_mm2(sq[:, c:c + PAIR], ones_bd) for c in range(0, d_b, PAIR)], axis=1)
    kk = kk * lax.rsqrt(jnp.maximum(ss, 1e-24))
    return r, lw, k * (1.0 + (a - 1.0) * k_a), v, kk, -(kk * a), g


def _wkv_chunk(r, lw, cum, k, v, p, q, g, rk, gng, gnb, S, consts):
    m0, bd_strict, sel_incl, bd_mask, eye, ones_bd, inv_masks = consts
    c2 = 2 * CHUNK
    pm = lambda f, *ls: [f(*a) for a in zip(*ls)]
    mm = _mm1
    zero = jnp.zeros((CHUNK, PAIR), F32)
    stack = lambda x: jnp.concatenate([jnp.where(m0, x, zero), jnp.where(m0, zero, x)], axis=0)
    pick = lambda blk: jnp.where(m0, blk[:CHUNK], blk[CHUNK:])

    cum_end = pm(lambda c: c[CHUNK - 1:CHUNK, :], cum)
    g_in = pm(jnp.exp, cum)
    g_ex = pm(lambda c, l: jnp.exp(c - l), cum, lw)
    g_inv = pm(lambda c: jnp.exp(-c), cum)
    g_bar = pm(lambda ce, c: jnp.exp(ce - c), cum_end, cum)
    mul = lambda a, b: a * b
    pt, rt = pm(mul, p, g_ex), pm(mul, r, g_in)
    qt, kt = pm(mul, q, g_inv), pm(mul, k, g_inv)
    qb, kb = pm(mul, q, g_bar), pm(mul, k, g_bar)

    a_all = pm(lambda a, b, c, d: mm(jnp.concatenate([stack(a), stack(b)], axis=0),
                                     jnp.concatenate([c, c, d, d], axis=0), NT), pt, rt, qt, kt)
    x_all = pm(lambda a, b, s: mm(jnp.concatenate([a, b], axis=0), s, NT), pt, rt, S)
    a_pq = pm(lambda a: jnp.where(bd_strict, a[:c2, :c2], 0.0), a_all)
    a_pk = pm(lambda a: jnp.where(bd_strict, a[:c2, c2:], 0.0), a_all)
    a_rq = pm(lambda a: jnp.where(sel_incl, pick(a[c2:, :c2]), 0.0), a_all)
    a_rk = pm(lambda a: jnp.where(sel_incl, pick(a[c2:, c2:]), 0.0), a_all)
    v_st = pm(stack, v)
    w0 = pm(mm, a_pk, v_st)

    t_acc = pm(lambda a: eye + jnp.where(inv_masks[0], a, 0.0), a_pq)
    for off in inv_masks[1:]:
        y = pm(lambda a, t: mm(jnp.where(off, a, 0.0), t), a_pq, t_acc)
        t_acc = pm(lambda t, yy: t + mm(t, yy), t_acc, y)

    u_st = pm(lambda t, x, w: mm(t, stack(x[:CHUNK]) + w), t_acc, x_all, w0)
    o = pm(lambda x, aq, ak, u, vs: x[CHUNK:] + mm(jnp.concatenate([aq, ak], axis=1),
                                                   jnp.concatenate([u, vs], axis=0)),
           x_all, a_rq, a_rk, u_st, v_st)
    upd = pm(lambda u, vv, a, b: mm(jnp.concatenate([u[:CHUNK] + u[CHUNK:], vv], axis=0).T,
                                    jnp.concatenate([a, b], axis=0)), u_st, v, qb, kb)
    s_new = pm(lambda s, ce, u: s * jnp.exp(ce) + jnp.where(bd_mask, u, 0.0), S, cum_end, upd)

    inv_n = 1.0 / HEAD
    mean = pm(lambda x: mm(x, ones_bd) * inv_n, o)
    oc = pm(lambda x, m: x - m, o, mean)
    var = pm(lambda x: mm(x * x, ones_bd) * inv_n, oc)
    bonus = pm(lambda a, b, c: mm(a * b * c, ones_bd), r, k, rk)
    y = pm(lambda x, s2, gg, gb, bo, vv, gt: (x * lax.rsqrt(s2 + GN_EPS) * gg + gb + bo * vv) * gt,
           oc, var, gng, gnb, bonus, v, g)
    return y, s_new


def _wkv_kernel(pj_ref, sh0_ref, mu_ref, w0_ref, a0_ref, kk_ref, ka_ref, wwa_ref, g2_ref, rk_ref, gng_ref, gnb_ref,
                s0_ref, y_ref, so_ref, s_scr, prev_scr, *, n_chunks, d_b):
    tb = pl.program_id(1)

    @pl.when(tb == 0)
    def _():
        s_scr[...] = s0_ref[...]
        prev_scr[...] = sh0_ref[...]

    lane = lax.broadcasted_iota(jnp.int32, (CHUNK, PAIR), 1)
    m0 = lane < HEAD
    c2 = 2 * CHUNK
    ri = lax.broadcasted_iota(jnp.int32, (c2, c2), 0)
    ci = lax.broadcasted_iota(jnp.int32, (c2, c2), 1)
    same = (ri // CHUNK) == (ci // CHUNK)
    bd_strict = same & ((ci % CHUNK) < (ri % CHUNK))
    bd_mask = (ri // HEAD) == (ci // HEAD)
    eye = jnp.where(ri == ci, 1.0, 0.0).astype(F32)
    ones_bd = jnp.where(bd_mask, 1.0, 0.0).astype(BF16)
    ti = lax.broadcasted_iota(jnp.int32, (CHUNK, c2), 0)
    si = lax.broadcasted_iota(jnp.int32, (CHUNK, c2), 1)
    sel_incl = (si % CHUNK) <= ti
    inv_masks = [(ri // 2) == (ci // 2)]
    blk = 2
    while blk < CHUNK:
        inv_masks.append(((ri // (2 * blk)) == (ci // (2 * blk))) & ((ri // blk) != (ci // blk)))
        blk *= 2
    consts = (m0, bd_strict, sel_incl, bd_mask, eye, ones_bd, inv_masks)
    tr = lax.broadcasted_iota(jnp.int32, (CHUNK, CHUNK), 0)
    tc = lax.broadcasted_iota(jnp.int32, (CHUNK, CHUNK), 1)
    tri = jnp.where(tc <= tr, 1.0, 0.0).astype(BF16)

    n_sb = pj_ref.shape[0]
    n_pairs = d_b // PAIR
    units = [(sb, pr, slice(pr * PAIR, (pr + 1) * PAIR)) for sb in range(n_sb) for pr in range(n_pairs)]

    def chunk_body(c, carry):
        rows = pl.ds(pl.multiple_of(c * CHUNK, CHUNK), CHUNK)
        per_sb = []
        for sb in range(n_sb):
            pb = pj_ref[sb, rows, :]
            r, lw, k, v, kk, q, g = _rwkv_inputs(
                pb, prev_scr[sb], mu_ref[...], w0_ref[...], a0_ref[...], kk_ref[...], ka_ref[...],
                wwa_ref[...], g2_ref[...], ones_bd, d_b)
            prev_scr[sb] = pb[CHUNK - 1:CHUNK, :]
            lw_hi, lw_lo = _split(lw)
            cum = _dot(tri, lw_hi) + _dot(tri, lw_lo)
            per_sb.append((r, lw, cum, k, v, kk, q, g))
        seq = lambda i: [per_sb[sb][i][:, ls] for sb, _, ls in units]
        vec = lambda ref: [ref[:, ls] for _, _, ls in units]
        y, s_new = _wkv_chunk(*[seq(i) for i in range(8)], vec(rk_ref), vec(gng_ref), vec(gnb_ref),
                              [s_scr[sb, pr] for sb, pr, _ in units], consts)
        for i, (sb, pr, ls) in enumerate(units):
            s_scr[sb, pr] = s_new[i]
            y_ref[sb, rows, ls] = y[i].astype(y_ref.dtype)
        return carry

    lax.fori_loop(0, n_chunks, chunk_body, 0)

    @pl.when(tb == pl.num_programs(1) - 1)
    def _():
        so_ref[...] = s_scr[...]


def wkv(pj, shift0, mu, w0, a0, k_k, k_a, wwa, g2p, rk, gng, gnb, s0_bd, t_len):
    rows, width = pj.shape
    d_b = w0.shape[1]
    n_streams = rows // t_len
    n_pairs = d_b // PAIR
    n_sb = WKV_STREAMS if n_streams % WKV_STREAMS == 0 else 1
    tb_len = _row_tile(t_len, 128)
    vec = lambda n: pl.BlockSpec((1, n), lambda s, t: (0, 0))
    full = lambda a: pl.BlockSpec(a.shape, lambda s, t: (0, 0))
    st = pl.BlockSpec((n_sb, n_pairs, PAIR, PAIR), lambda s, t: (s, 0, 0, 0))
    y, s_out = pl.pallas_call(
        functools.partial(_wkv_kernel, n_chunks=tb_len // CHUNK, d_b=d_b),
        grid=(n_streams // n_sb, t_len // tb_len),
        in_specs=[
            pl.BlockSpec((n_sb, tb_len, width), lambda s, t: (s, t, 0)),
            pl.BlockSpec((n_sb, 1, width), lambda s, t: (s, 0, 0)),
            vec(width), vec(d_b), vec(d_b), vec(d_b), vec(d_b),
            full(wwa), full(g2p), vec(d_b), vec(d_b), vec(d_b), st,
        ],
        out_specs=[pl.BlockSpec((n_sb, tb_len, d_b), lambda s, t: (s, t, 0)), st],
        out_shape=[
            jax.ShapeDtypeStruct((n_streams, t_len, d_b), BF16),
            jax.ShapeDtypeStruct(s0_bd.shape, F32),
        ],
        scratch_shapes=[pltpu.VMEM((n_sb, n_pairs, PAIR, PAIR), F32), pltpu.VMEM((n_sb, 1, width), F32)],
        compiler_params=_params(("parallel", "arbitrary")),
        name="wkv",
    )(pj.reshape(n_streams, t_len, width), shift0, mu, w0, a0, k_k, k_a, wwa, g2p, rk, gng, gnb, s0_bd)
    return y.reshape(rows, d_b), s_out


def _mix_out_kernel(x_ref, ya_ref, yb_ref, ga_ref, gb_ref, wa_ref, wb_ref, wo_ref, o_ref):
    gate = lambda ref: 0.5 + 0.5 * jnp.tanh(0.5 * ref[...].astype(F32))
    mixed = gate(ga_ref) * _dot(ya_ref[...], wa_ref[...]) + gate(gb_ref) * _dot(yb_ref[...], wb_ref[...])
    o_ref[...] = x_ref[...] + _dot(mixed.astype(BF16), wo_ref[...])


def mix_out(x, ya, yb, gates, w_pa, w_pb, w_o, layer):
    rows, d = x.shape
    d_a, d_b = ya.shape[1], yb.shape[1]
    tm = _row_tile(rows, 256)
    resident = lambda a: pl.BlockSpec((None,) + a.shape[1:], lambda i: (layer, 0, 0), pipeline_mode=pl.Buffered(1))
    return pl.pallas_call(
        _mix_out_kernel,
        grid=(rows // tm,),
        in_specs=[
            pl.BlockSpec((tm, d), lambda i: (i, 0)),
            pl.BlockSpec((tm, d_a), lambda i: (i, 0)),
            pl.BlockSpec((tm, d_b), lambda i: (i, 0)),
            pl.BlockSpec((tm, d), lambda i: (i, 0)),
            pl.BlockSpec((tm, d), lambda i: (i, 1)),
            resident(w_pa), resident(w_pb), resident(w_o),
        ],
        out_specs=pl.BlockSpec((tm, d), lambda i: (i, 0)),
        out_shape=jax.ShapeDtypeStruct((rows, d), F32),
        compiler_params=_params(("parallel",)),
        name="mix_out",
    )(x, ya, yb, gates, gates, w_pa, w_pb, w_o)


def _ffn_kernel(x_ref, g_ref, wu_ref, wd_ref, gf_ref, o_ref, h_ref, acc_ref, *, final_norm):
    f = pl.program_id(1)

    @pl.when(f == 0)
    def _():
        h_ref[...] = _rms(x_ref[...], g_ref[...]).astype(BF16)
        acc_ref[...] = jnp.zeros_like(acc_ref)

    up = jnp.maximum(_dot(h_ref[...], wu_ref[...]), 0.0)
    acc_ref[...] += _dot((up * up).astype(BF16), wd_ref[...])

    @pl.when(f == pl.num_programs(1) - 1)
    def _():
        y = x_ref[...] + acc_ref[...]
        if final_norm:
            y = _rms(y, gf_ref[...])
        o_ref[...] = y


def ffn(x, gain, w_up, w_down, layer, final_gain, final_norm, tf=1024):
    rows, d = x.shape
    d_ff = w_up.shape[2]
    tm = _row_tile(rows, 512)
    return pl.pallas_call(
        functools.partial(_ffn_kernel, final_norm=final_norm),
        grid=(rows // tm, d_ff // tf),
        in_specs=[
            pl.BlockSpec((tm, d), lambda i, f: (i, 0)),
            pl.BlockSpec((1, d), lambda i, f: (0, 0)),
            pl.BlockSpec((None, d, tf), lambda i, f: (layer, 0, f)),
            pl.BlockSpec((None, tf, d), lambda i, f: (layer, f, 0)),
            pl.BlockSpec((1, d), lambda i, f: (0, 0)),
        ],
        out_specs=pl.BlockSpec((tm, d), lambda i, f: (i, 0)),
        out_shape=jax.ShapeDtypeStruct((rows, d), F32),
        scratch_shapes=[pltpu.VMEM((tm, d), BF16), pltpu.VMEM((tm, d), F32)],
        compiler_params=_params(("parallel", "arbitrary")),
        name="ffn",
    )(x, gain, w_up, w_down, final_gain)


def _to_block_diag(s):
    b, h, n, _ = s.shape
    sr = s.reshape(b, h // 2, 2, n, n)
    z = jnp.zeros_like(sr[:, :, 0])
    top = jnp.concatenate([sr[:, :, 0], z], axis=-1)
    bot = jnp.concatenate([z, sr[:, :, 1]], axis=-1)
    return jnp.concatenate([top, bot], axis=-2)


def _from_block_diag(sbd):
    b, hp, n2, _ = sbd.shape
    n = n2 // 2
    return jnp.stack([sbd[:, :, :n, :n], sbd[:, :, n:, n:]], axis=2).reshape(b, 2 * hp, n, n)


def _layer(x, shift0, s0, t_len, wt, big, layer, final_gain, final_norm, want_vn):
    rows = x.shape[0]
    n_streams = rows // t_len
    d_a = wt["ln_g"].shape[1]
    d_shift = shift0.shape[1]
    width = wt["mu"].shape[1]

    uv, pj, gates = proj(x, wt["norm1"], big["w_cat"], layer, 2 * d_a, width)

    blk = MLP_BLOCK if t_len % MLP_BLOCK == 0 else t_len
    ya, *vn = gmlp(uv, wt["ln_g"], wt["ln_b"], wt["w_s"][:, :blk, :blk], wt["b_full"][:, :blk], blk, want_vn)

    shift0p = jnp.pad(shift0, ((0, 0), (0, width - d_shift)))[:, None, :]
    yb, s_bd = wkv(pj, shift0p, wt["mu"], wt["w0"], wt["a0"], wt["k_k"], wt["k_a"], wt["wwa"], wt["g2p"],
                   wt["r_k"], wt["gn_g"], wt["gn_b"], _to_block_diag(s0), t_len)

    x = mix_out(x, ya, yb, gates, big["w_pa"], big["w_pb"], big["w_o"], layer)
    x = ffn(x, wt["norm2"], big["w_up"], big["w_down"], layer, final_gain, final_norm)

    new_shift = pj.reshape(n_streams, t_len, width)[:, -1, :d_shift]
    return x, (vn[0] if want_vn else None), new_shift, _from_block_diag(s_bd)


def kernel(x_prompt, x_sample, state_tshift, state_wkv, norm1, w_in, ln_v_g, ln_v_b, w_s, b_s, mu_shift,
           w0, w2, a0, a2, g2, k_k, k_a, r_k, gn_g, gn_b, w_gate, w_pa, w_pb, w_o, norm2, w_up, w_down,
           norm_f):
    depth = w_in.shape[0]
    bp, tp, d = x_prompt.shape
    bs, ts, _ = x_sample.shape
    d_a = ln_v_g.shape[1]
    d_b = w0.shape[1]
    d_shift = mu_shift.shape[1]
    heads = d_b // HEAD
    assert tp % CHUNK == 0 and ts % CHUNK == 0 and heads % 2 == 0
    assert all(t % MLP_BLOCK == 0 or (t < MLP_BLOCK and t % 8 == 0) for t in (tp, ts))
    lo_w, lo_a = w2.shape[1], a2.shape[1]
    assert lo_w == HEAD and lo_a == HEAD
    lo_g = g2.shape[1]
    width = -(-d_shift // PROJ_TN) * PROJ_TN
    g_w = width - (3 * d_b + PAIR)
    dg = d_a // w_s.shape[1]

    row = lambda a: a.reshape(1, -1)

    xp = x_prompt.reshape(bp * tp, d)
    xs = x_sample.reshape(bs * ts, d)
    zero_shift = jnp.zeros((bp, d_shift), F32)
    zero_wkv = jnp.zeros((bp, heads, HEAD, HEAD), F32)
    outs = {n: [] for n in ("tsh_p", "wkv_p", "tsh_s", "wkv_s", "vrow_s")}
    big = {
        "w_cat": jnp.concatenate(
            [w_in.astype(BF16), jnp.zeros((depth, d, width - d_shift), BF16), w_gate.astype(BF16)], axis=2),
        "w_pa": w_pa.astype(BF16), "w_pb": w_pb.astype(BF16), "w_o": w_o.astype(BF16),
        "w_up": w_up.astype(BF16), "w_down": w_down.astype(BF16),
    }
    for l in range(depth):
        wwa = jnp.zeros((PAIR, 2 * d_b), F32)
        wwa = wwa.at[:lo_w, :d_b].set(w2[l]).at[lo_w:, d_b:].set(a2[l])
        wt = {
            "norm1": row(norm1[l]), "norm2": row(norm2[l]),
            "ln_g": row(ln_v_g[l]), "ln_b": row(ln_v_b[l]),
            "w_s": w_s[l],
            "b_full": jnp.broadcast_to(b_s[l][:, :, None], b_s[l].shape + (dg,)),
            "mu": row(jnp.pad(mu_shift[l], (0, width - d_shift))),
            "w0": row(w0[l]), "a0": row(a0[l]), "k_k": row(k_k[l]), "k_a": row(k_a[l]),
            "wwa": wwa.astype(BF16),
            "g2p": jnp.pad(g2[l], ((0, g_w - lo_g), (0, 0))).astype(BF16),
            "r_k": row(r_k[l]), "gn_g": row(gn_g[l]), "gn_b": row(gn_b[l]),
        }
        last = l == depth - 1
        gf = row(norm_f)
        xp, _, shp, wkp = _layer(xp, zero_shift, zero_wkv, tp, wt, big, l, gf, last, False)
        xs, vrows, shs, wks = _layer(xs, state_tshift[l], state_wkv[l], ts, wt, big, l, gf, last, True)
        outs["tsh_p"].append(shp)
        outs["wkv_p"].append(wkp)
        outs["tsh_s"].append(shs)
        outs["wkv_s"].append(wks)
        outs["vrow_s"].append(vrows.reshape(bs, ts, d_a))
    return (xp.reshape(bp, tp, d), xs.reshape(bs, ts, d), jnp.stack(outs["tsh_p"]), jnp.stack(outs["wkv_p"]),
            jnp.stack(outs["tsh_s"]), jnp.stack(outs["wkv_s"]), jnp.stack(outs["vrow_s"]))
```

```python
import functools

import jax
import jax.numpy as jnp
from jax import lax
from jax.experimental import pallas as pl
from jax.experimental.pallas import tpu as pltpu

F32 = jnp.float32
BF16 = jnp.bfloat16

HEAD = 64
PAIR = 2 * HEAD
CHUNK = 64
WKV_STREAMS = 2
MLP_BLOCK = 128
PROJ_TN = 512
MXU_COLS = 256
RMS_EPS = 1e-5
LN_EPS = 1e-5
GN_EPS = 64e-5
DECAY_SCALE = 0.6065306597126334
VMEM_LIMIT = 56 * 1024 * 1024

NN = (((1,), (0,)), ((), ()))
NT = (((1,), (1,)), ((), ()))


def _params(sem):
    return pltpu.CompilerParams(dimension_semantics=sem, vmem_limit_bytes=VMEM_LIMIT)


def _dot(a, b, dims=NN):
    return lax.dot_general(a, b, dims, preferred_element_type=F32)


def _split(x):
    hi = x.astype(BF16)
    lo = (x - hi.astype(F32)).astype(BF16)
    return hi, lo


def _mm1(a, b, dims=NN):
    return _dot(a.astype(BF16), b.astype(BF16), dims)


def _mm2(a, b_exact):
    ah, al = _split(a)
    return _dot(ah, b_exact) + _dot(al, b_exact)


def _rms(x, g):
    return x * lax.rsqrt(jnp.mean(x * x, axis=-1, keepdims=True) + RMS_EPS) * g


def _gelu(x):
    return 0.5 * x * (1.0 + jnp.tanh(0.7978845608028654 * (x + 0.044715 * (x * x * x))))


def _sigmoid(x):
    return 1.0 / (1.0 + jnp.exp(-x))


def _identity(x):
    return x


def _row_tile(rows, cap):
    t = min(rows, cap)
    while rows % t:
        t //= 2
    return t


def _proj_kernel(x_ref, g_ref, w_ref, uv_ref, pj_ref, gt_ref, h_ref, *, n_uv, n_pj):
    j = pl.program_id(1)

    @pl.when(j == 0)
    def _():
        h_ref[...] = _rms(x_ref[...], g_ref[...]).astype(BF16)

    def emit(o_ref, act):
        for c in range(0, o_ref.shape[1], MXU_COLS):
            o_ref[:, c:c + MXU_COLS] = act(_dot(h_ref[...], w_ref[:, c:c + MXU_COLS])).astype(o_ref.dtype)

    @pl.when(j < n_uv)
    def _():
        emit(uv_ref, _gelu)

    @pl.when((j >= n_uv) & (j < n_uv + n_pj))
    def _():
        emit(pj_ref, _identity)

    @pl.when(j >= n_uv + n_pj)
    def _():
        emit(gt_ref, _identity)


def proj(x, gain, w_cat, layer, n_uv_cols, n_pj_cols):
    rows, d = x.shape
    n = w_cat.shape[2]
    tn = PROJ_TN
    n_uv, n_pj = n_uv_cols // tn, n_pj_cols // tn
    n_gt = n // tn - n_uv - n_pj
    tm = _row_tile(rows, 1024)
    return pl.pallas_call(
        functools.partial(_proj_kernel, n_uv=n_uv, n_pj=n_pj),
        grid=(rows // tm, n // tn),
        in_specs=[
            pl.BlockSpec((tm, d), lambda i, j: (i, 0)),
            pl.BlockSpec((1, d), lambda i, j: (0, 0)),
            pl.BlockSpec((None, d, tn), lambda i, j: (layer, 0, j)),
        ],
        out_specs=[
            pl.BlockSpec((tm, tn), lambda i, j: (i, jnp.minimum(j, n_uv - 1))),
            pl.BlockSpec((tm, tn), lambda i, j: (i, jnp.clip(j - n_uv, 0, n_pj - 1))),
            pl.BlockSpec((tm, tn), lambda i, j: (i, jnp.clip(j - n_uv - n_pj, 0, n_gt - 1))),
        ],
        out_shape=[
            jax.ShapeDtypeStruct((rows, n_uv * tn), F32),
            jax.ShapeDtypeStruct((rows, n_pj * tn), F32),
            jax.ShapeDtypeStruct((rows, n_gt * tn), BF16),
        ],
        scratch_shapes=[pltpu.VMEM((tm, d), BF16)],
        compiler_params=_params(("parallel", "arbitrary")),
        name="proj",
    )(x, gain, w_cat)


def _gmlp_rows(u_ref, v_ref, ln_g, ln_b, ws_ref, bs_ref, blk):
    v = v_ref[...]
    mu = jnp.mean(v, axis=-1, keepdims=True)
    vc = v - mu
    var = jnp.mean(vc * vc, axis=-1, keepdims=True)
    vn = vc * lax.rsqrt(var + LN_EPS) * ln_g + ln_b
    vnb = vn.astype(BF16)
    q_idx = lax.broadcasted_iota(jnp.int32, (blk, blk), 0)
    k_idx = lax.broadcasted_iota(jnp.int32, (blk, blk), 1)
    causal = k_idx <= q_idx
    groups = ws_ref.shape[0]
    dg = v.shape[1] // groups
    cols = []
    for g in range(groups):
        wg = jnp.where(causal, ws_ref[g], 0.0).astype(BF16)
        bg = bs_ref[g]
        cs = slice(g * dg, (g + 1) * dg)
        parts = [(u_ref[c:c + blk, cs] * (_dot(wg, vnb[c:c + blk, cs]) + bg)).astype(BF16)
                 for c in range(0, v.shape[0], blk)]
        cols.append(jnp.concatenate(parts, axis=0))
    return jnp.concatenate(cols, axis=1), vn


def _rwkv_inputs(pb, prev_row, mu, w0, a0, k_k, k_a, wwa, g2, ones_bd, d_b):
    row = lax.broadcasted_iota(jnp.int32, pb.shape, 0)
    prev = jnp.where(row == 0, prev_row, pltpu.roll(pb, 1, 0))
    xs = pb + (prev - pb) * mu
    r = xs[:, 0:d_b]
    k = xs[:, d_b:2 * d_b]
    v = xs[:, 2 * d_b:3 * d_b]
    wa = xs[:, 3 * d_b:3 * d_b + PAIR]
    gl = xs[:, 3 * d_b + PAIR:]
    lane = lax.broadcasted_iota(jnp.int32, wa.shape, 1)
    lora = _mm1(jnp.where(lane < HEAD, jnp.tanh(wa), wa), wwa)
    lw = -DECAY_SCALE * _sigmoid(w0 + lora[:, :d_b])
    a = _sigmoid(a0 + lora[:, d_b:])
    g = _mm1(_sigmoid(gl), g2)
    kk = k * k_k
    sq = kk * kk
    ss = jnp.concatenate([_mm2(sq[:, c:c + PAIR], ones_bd) for c in range(0, d_b, PAIR)], axis=1)
    kk = kk * lax.rsqrt(jnp.maximum(ss, 1e-24))
    return r, lw, k * (1.0 + (a - 1.0) * k_a), v, kk, -(kk * a), g


def _wkv_chunk(r, lw, cum, k, v, p, q, g, rk, gng, gnb, S, consts):
    m0, bd_strict, sel_incl, bd_mask, eye, ones_bd, inv_masks = consts
    c2 = 2 * CHUNK
    pm = lambda f, *ls: [f(*a) for a in zip(*ls)]
    mm = _mm1
    zero = jnp.zeros((CHUNK, PAIR), F32)
    stack = lambda x: jnp.concatenate([jnp.where(m0, x, zero), jnp.where(m0, zero, x)], axis=0)
    pick = lambda blk: jnp.where(m0, blk[:CHUNK], blk[CHUNK:])

    cum_end = pm(lambda c: c[CHUNK - 1:CHUNK, :], cum)
    g_in = pm(jnp.exp, cum)
    g_ex = pm(lambda c, l: jnp.exp(c - l), cum, lw)
    g_inv = pm(lambda c: jnp.exp(-c), cum)
    g_bar = pm(lambda ce, c: jnp.exp(ce - c), cum_end, cum)
    mul = lambda a, b: a * b
    pt, rt = pm(mul, p, g_ex), pm(mul, r, g_in)
    qt, kt = pm(mul, q, g_inv), pm(mul, k, g_inv)
    qb, kb = pm(mul, q, g_bar), pm(mul, k, g_bar)

    a_all = pm(lambda a, b, c, d: mm(jnp.concatenate([stack(a), stack(b)], axis=0),
                                     jnp.concatenate([c, c, d, d], axis=0), NT), pt, rt, qt, kt)
    x_all = pm(lambda a, b, s: mm(jnp.concatenate([a, b], axis=0), s, NT), pt, rt, S)
    a_pq = pm(lambda a: jnp.where(bd_strict, a[:c2, :c2], 0.0), a_all)
    a_pk = pm(lambda a: jnp.where(bd_strict, a[:c2, c2:], 0.0), a_all)
    a_rq = pm(lambda a: jnp.where(sel_incl, pick(a[c2:, :c2]), 0.0), a_all)
    a_rk = pm(lambda a: jnp.where(sel_incl, pick(a[c2:, c2:]), 0.0), a_all)
    v_st = pm(stack, v)
    w0 = pm(mm, a_pk, v_st)

    t_acc = pm(lambda a: eye + jnp.where(inv_masks[0], a, 0.0), a_pq)
    for off in inv_masks[1:]:
        y = pm(lambda a, t: mm(jnp.where(off, a, 0.0), t), a_pq, t_acc)
        t_acc = pm(lambda t, yy: t + mm(t, yy), t_acc, y)

    u_st = pm(lambda t, x, w: mm(t, stack(x[:CHUNK]) + w), t_acc, x_all, w0)
    o = pm(lambda x, aq, ak, u, vs: x[CHUNK:] + mm(jnp.concatenate([aq, ak], axis=1),
                                                   jnp.concatenate([u, vs], axis=0)),
           x_all, a_rq, a_rk, u_st, v_st)
    upd = pm(lambda u, vv, a, b: mm(jnp.concatenate([u[:CHUNK] + u[CHUNK:], vv], axis=0).T,
                                    jnp.concatenate([a, b], axis=0)), u_st, v, qb, kb)
    s_new = pm(lambda s, ce, u: s * jnp.exp(ce) + jnp.where(bd_mask, u, 0.0), S, cum_end, upd)

    inv_n = 1.0 / HEAD
    mean = pm(lambda x: mm(x, ones_bd) * inv_n, o)
    oc = pm(lambda x, m: x - m, o, mean)
    var = pm(lambda x: mm(x * x, ones_bd) * inv_n, oc)
    bonus = pm(lambda a, b, c: mm(a * b * c, ones_bd), r, k, rk)
    y = pm(lambda x, s2, gg, gb, bo, vv, gt: (x * lax.rsqrt(s2 + GN_EPS) * gg + gb + bo * vv) * gt,
           oc, var, gng, gnb, bonus, v, g)
    return y, s_new


def _wkv_kernel(pj_ref, sh0_ref, mu_ref, w0_ref, a0_ref, kk_ref, ka_ref, wwa_ref, g2_ref, rk_ref, gng_ref, gnb_ref,
                s0_ref, y_ref, so_ref, s_scr, prev_scr, *, n_chunks, d_b):
    tb = pl.program_id(1)

    @pl.when(tb == 0)
    def _():
        s_scr[...] = s0_ref[...]
        prev_scr[...] = sh0_ref[...]

    lane = lax.broadcasted_iota(jnp.int32, (CHUNK, PAIR), 1)
    m0 = lane < HEAD
    c2 = 2 * CHUNK
    ri = lax.broadcasted_iota(jnp.int32, (c2, c2), 0)
    ci = lax.broadcasted_iota(jnp.int32, (c2, c2), 1)
    same = (ri // CHUNK) == (ci // CHUNK)
    bd_strict = same & ((ci % CHUNK) < (ri % CHUNK))
    bd_mask = (ri // HEAD) == (ci // HEAD)
    eye = jnp.where(ri == ci, 1.0, 0.0).astype(F32)
    ones_bd = jnp.where(bd_mask, 1.0, 0.0).astype(BF16)
    ti = lax.broadcasted_iota(jnp.int32, (CHUNK, c2), 0)
    si = lax.broadcasted_iota(jnp.int32, (CHUNK, c2), 1)
    sel_incl = (si % CHUNK) <= ti
    inv_masks = [(ri // 2) == (ci // 2)]
    blk = 2
    while blk < CHUNK:
        inv_masks.append(((ri // (2 * blk)) == (ci // (2 * blk))) & ((ri // blk) != (ci // blk)))
        blk *= 2
    consts = (m0, bd_strict, sel_incl, bd_mask, eye, ones_bd, inv_masks)
    tr = lax.broadcasted_iota(jnp.int32, (CHUNK, CHUNK), 0)
    tc = lax.broadcasted_iota(jnp.int32, (CHUNK, CHUNK), 1)
    tri = jnp.where(tc <= tr, 1.0, 0.0).astype(BF16)

    n_sb = pj_ref.shape[0]
    n_pairs = d_b // PAIR
    units = [(sb, pr, slice(pr * PAIR, (pr + 1) * PAIR)) for sb in range(n_sb) for pr in range(n_pairs)]

    def chunk_body(c, carry):
        rows = pl.ds(pl.multiple_of(c * CHUNK, CHUNK), CHUNK)
        per_sb = []
        for sb in range(n_sb):
            pb = pj_ref[sb, rows, :]
            r, lw, k, v, kk, q, g = _rwkv_inputs(
                pb, prev_scr[sb], mu_ref[...], w0_ref[...], a0_ref[...], kk_ref[...], ka_ref[...],
                wwa_ref[...], g2_ref[...], ones_bd, d_b)
            prev_scr[sb] = pb[CHUNK - 1:CHUNK, :]
            lw_hi, lw_lo = _split(lw)
            cum = _dot(tri, lw_hi) + _dot(tri, lw_lo)
            per_sb.append((r, lw, cum, k, v, kk, q, g))
        seq = lambda i: [per_sb[sb][i][:, ls] for sb, _, ls in units]
        vec = lambda ref: [ref[:, ls] for _, _, ls in units]
        y, s_new = _wkv_chunk(*[seq(i) for i in range(8)], vec(rk_ref), vec(gng_ref), vec(gnb_ref),
                              [s_scr[sb, pr] for sb, pr, _ in units], consts)
        for i, (sb, pr, ls) in enumerate(units):
            s_scr[sb, pr] = s_new[i]
            y_ref[sb, rows, ls] = y[i].astype(y_ref.dtype)
        return carry

    lax.fori_loop(0, n_chunks, chunk_body, 0)

    @pl.when(tb == pl.num_programs(1) - 1)
    def _():
        so_ref[...] = s_scr[...]


def wkv(pj, shift0, mu, w0, a0, k_k, k_a, wwa, g2p, rk, gng, gnb, s0_bd, t_len):
    rows, width = pj.shape
    d_b = w0.shape[1]
    n_streams = rows // t_len
    n_pairs = d_b // PAIR
    n_sb = WKV_STREAMS if n_streams % WKV_STREAMS == 0 else 1
    tb_len = _row_tile(t_len, 128)
    vec = lambda n: pl.BlockSpec((1, n), lambda s, t: (0, 0))
    full = lambda a: pl.BlockSpec(a.shape, lambda s, t: (0, 0))
    st = pl.BlockSpec((n_sb, n_pairs, PAIR, PAIR), lambda s, t: (s, 0, 0, 0))
    y, s_out = pl.pallas_call(
        functools.partial(_wkv_kernel, n_chunks=tb_len // CHUNK, d_b=d_b),
        grid=(n_streams // n_sb, t_len // tb_len),
        in_specs=[
            pl.BlockSpec((n_sb, tb_len, width), lambda s, t: (s, t, 0)),
            pl.BlockSpec((n_sb, 1, width), lambda s, t: (s, 0, 0)),
            vec(width), vec(d_b), vec(d_b), vec(d_b), vec(d_b),
            full(wwa), full(g2p), vec(d_b), vec(d_b), vec(d_b), st,
        ],
        out_specs=[pl.BlockSpec((n_sb, tb_len, d_b), lambda s, t: (s, t, 0)), st],
        out_shape=[
            jax.ShapeDtypeStruct((n_streams, t_len, d_b), BF16),
            jax.ShapeDtypeStruct(s0_bd.shape, F32),
        ],
        scratch_shapes=[pltpu.VMEM((n_sb, n_pairs, PAIR, PAIR), F32), pltpu.VMEM((n_sb, 1, width), F32)],
        compiler_params=_params(("parallel", "arbitrary")),
        name="wkv",
    )(pj.reshape(n_streams, t_len, width), shift0, mu, w0, a0, k_k, k_a, wwa, g2p, rk, gng, gnb, s0_bd)
    return y.reshape(rows, d_b), s_out


def _mix_out_kernel(x_ref, u_ref, v_ref, yb_ref, ga_ref, gb_ref, lg_ref, lb_ref, ws_ref, bs_ref, wa_ref, wb_ref,
                    wo_ref, o_ref, *maybe_vn_ref, blk):
    ya, vn = _gmlp_rows(u_ref, v_ref, lg_ref[...], lb_ref[...], ws_ref, bs_ref, blk)
    for vn_ref in maybe_vn_ref:
        vn_ref[...] = vn
    gate = lambda ref: 0.5 + 0.5 * jnp.tanh(0.5 * ref[...].astype(F32))
    mixed = gate(ga_ref) * _dot(ya, wa_ref[...]) + gate(gb_ref) * _dot(yb_ref[...], wb_ref[...])
    o_ref[...] = x_ref[...] + _dot(mixed.astype(BF16), wo_ref[...])


def mix_out(x, uv, yb, gates, ln_g, ln_b, ws, bs_full, w_pa, w_pb, w_o, layer, blk, want_vn):
    rows, d = x.shape
    d_a, d_b = uv.shape[1] // 2, yb.shape[1]
    groups = ws.shape[0]
    tm = _row_tile(rows, max(blk, 256))
    n_out = 2 if want_vn else 1
    resident = lambda a: pl.BlockSpec((None,) + a.shape[1:], lambda i: (layer, 0, 0), pipeline_mode=pl.Buffered(1))
    return pl.pallas_call(
        functools.partial(_mix_out_kernel, blk=blk),
        grid=(rows // tm,),
        in_specs=[
            pl.BlockSpec((tm, d), lambda i: (i, 0)),
            pl.BlockSpec((tm, d_a), lambda i: (i, 0)),
            pl.BlockSpec((tm, d_a), lambda i: (i, 1)),
            pl.BlockSpec((tm, d_b), lambda i: (i, 0)),
            pl.BlockSpec((tm, d), lambda i: (i, 0)),
            pl.BlockSpec((tm, d), lambda i: (i, 1)),
            pl.BlockSpec((1, d_a), lambda i: (0, 0)),
            pl.BlockSpec((1, d_a), lambda i: (0, 0)),
            pl.BlockSpec((groups, blk, blk), lambda i: (0, 0, 0)),
            pl.BlockSpec((groups, blk, d_a // groups), lambda i: (0, 0, 0)),
            resident(w_pa), resident(w_pb), resident(w_o),
        ],
        out_specs=[pl.BlockSpec((tm, d), lambda i: (i, 0)), pl.BlockSpec((tm, d_a), lambda i: (i, 0))][:n_out],
        out_shape=[jax.ShapeDtypeStruct((rows, d), F32), jax.ShapeDtypeStruct((rows, d_a), F32)][:n_out],
        compiler_params=_params(("parallel",)),
        name="mix_out",
    )(x, uv, uv, yb, gates, gates, ln_g, ln_b, ws, bs_full, w_pa, w_pb, w_o)


def _ffn_kernel(x_ref, g_ref, wu_ref, wd_ref, gf_ref, o_ref, h_ref, acc_ref, *, final_norm):
    f = pl.program_id(1)

    @pl.when(f == 0)
    def _():
        h_ref[...] = _rms(x_ref[...], g_ref[...]).astype(BF16)
        acc_ref[...] = jnp.zeros_like(acc_ref)

    up = jnp.maximum(_dot(h_ref[...], wu_ref[...]), 0.0)
    acc_ref[...] += _dot((up * up).astype(BF16), wd_ref[...])

    @pl.when(f == pl.num_programs(1) - 1)
    def _():
        y = x_ref[...] + acc_ref[...]
        if final_norm:
            y = _rms(y, gf_ref[...])
        o_ref[...] = y


def ffn(x, gain, w_up, w_down, layer, final_gain, final_norm, tf=1024):
    rows, d = x.shape
    d_ff = w_up.shape[2]
    tm = _row_tile(rows, 512)
    return pl.pallas_call(
        functools.partial(_ffn_kernel, final_norm=final_norm),
        grid=(rows // tm, d_ff // tf),
        in_specs=[
            pl.BlockSpec((tm, d), lambda i, f: (i, 0)),
            pl.BlockSpec((1, d), lambda i, f: (0, 0)),
            pl.BlockSpec((None, d, tf), lambda i, f: (layer, 0, f)),
            pl.BlockSpec((None, tf, d), lambda i, f: (layer, f, 0)),
            pl.BlockSpec((1, d), lambda i, f: (0, 0)),
        ],
        out_specs=pl.BlockSpec((tm, d), lambda i, f: (i, 0)),
        out_shape=jax.ShapeDtypeStruct((rows, d), F32),
        scratch_shapes=[pltpu.VMEM((tm, d), BF16), pltpu.VMEM((tm, d), F32)],
        compiler_params=_params(("parallel", "arbitrary")),
        name="ffn",
    )(x, gain, w_up, w_down, final_gain)


def _to_block_diag(s):
    b, h, n, _ = s.shape
    sr = s.reshape(b, h // 2, 2, n, n)
    z = jnp.zeros_like(sr[:, :, 0])
    top = jnp.concatenate([sr[:, :, 0], z], axis=-1)
    bot = jnp.concatenate([z, sr[:, :, 1]], axis=-1)
    return jnp.concatenate([top, bot], axis=-2)


def _from_block_diag(sbd):
    b, hp, n2, _ = sbd.shape
    n = n2 // 2
    return jnp.stack([sbd[:, :, :n, :n], sbd[:, :, n:, n:]], axis=2).reshape(b, 2 * hp, n, n)


def _layer(x, shift0, s0, t_len, wt, big, layer, final_gain, final_norm, want_vn):
    rows = x.shape[0]
    n_streams = rows // t_len
    d_a = wt["ln_g"].shape[1]
    d_shift = shift0.shape[1]
    width = wt["mu"].shape[1]

    uv, pj, gates = proj(x, wt["norm1"], big["w_cat"], layer, 2 * d_a, width)

    shift0p = jnp.pad(shift0, ((0, 0), (0, width - d_shift)))[:, None, :]
    yb, s_bd = wkv(pj, shift0p, wt["mu"], wt["w0"], wt["a0"], wt["k_k"], wt["k_a"], wt["wwa"], wt["g2p"],
                   wt["r_k"], wt["gn_g"], wt["gn_b"], _to_block_diag(s0), t_len)

    blk = MLP_BLOCK if t_len % MLP_BLOCK == 0 else t_len
    x, *vn = mix_out(x, uv, yb, gates, wt["ln_g"], wt["ln_b"], wt["w_s"][:, :blk, :blk], wt["b_full"][:, :blk],
                     big["w_pa"], big["w_pb"], big["w_o"], layer, blk, want_vn)
    x = ffn(x, wt["norm2"], big["w_up"], big["w_down"], layer, final_gain, final_norm)

    new_shift = pj.reshape(n_streams, t_len, width)[:, -1, :d_shift]
    return x, (vn[0] if want_vn else None), new_shift, _from_block_diag(s_bd)


def kernel(x_prompt, x_sample, state_tshift, state_wkv, norm1, w_in, ln_v_g, ln_v_b, w_s, b_s, mu_shift,
           w0, w2, a0, a2, g2, k_k, k_a, r_k, gn_g, gn_b, w_gate, w_pa, w_pb, w_o, norm2, w_up, w_down,
           norm_f):
    depth = w_in.shape[0]
    bp, tp, d = x_prompt.shape
    bs, ts, _ = x_sample.shape
    d_a = ln_v_g.shape[1]
    d_b = w0.shape[1]
    d_shift = mu_shift.shape[1]
    heads = d_b // HEAD
    assert tp % CHUNK == 0 and ts % CHUNK == 0 and heads % 2 == 0
    assert all(t % MLP_BLOCK == 0 or (t < MLP_BLOCK and t % 8 == 0) for t in (tp, ts))
    lo_w, lo_a = w2.shape[1], a2.shape[1]
    assert lo_w == HEAD and lo_a == HEAD
    lo_g = g2.shape[1]
    width = -(-d_shift // PROJ_TN) * PROJ_TN
    g_w = width - (3 * d_b + PAIR)
    dg = d_a // w_s.shape[1]

    row = lambda a: a.reshape(1, -1)

    xp = x_prompt.reshape(bp * tp, d)
    xs = x_sample.reshape(bs * ts, d)
    zero_shift = jnp.zeros((bp, d_shift), F32)
    zero_wkv = jnp.zeros((bp, heads, HEAD, HEAD), F32)
    outs = {n: [] for n in ("tsh_p", "wkv_p", "tsh_s", "wkv_s", "vrow_s")}
    big = {
        "w_cat": jnp.concatenate(
            [w_in.astype(BF16), jnp.zeros((depth, d, width - d_shift), BF16), w_gate.astype(BF16)], axis=2),
        "w_pa": w_pa.astype(BF16), "w_pb": w_pb.astype(BF16), "w_o": w_o.astype(BF16),
        "w_up": w_up.astype(BF16), "w_down": w_down.astype(BF16),
    }
    for l in range(depth):
        wwa = jnp.zeros((PAIR, 2 * d_b), F32)
        wwa = wwa.at[:lo_w, :d_b].set(w2[l]).at[lo_w:, d_b:].set(a2[l])
        wt = {
            "norm1": row(norm1[l]), "norm2": row(norm2[l]),
            "ln_g": row(ln_v_g[l]), "ln_b": row(ln_v_b[l]),
            "w_s": w_s[l],
            "b_full": jnp.broadcast_to(b_s[l][:, :, None], b_s[l].shape + (dg,)),
            "mu": row(jnp.pad(mu_shift[l], (0, width - d_shift))),
            "w0": row(w0[l]), "a0": row(a0[l]), "k_k": row(k_k[l]), "k_a": row(k_a[l]),
            "wwa": wwa.astype(BF16),
            "g2p": jnp.pad(g2[l], ((0, g_w - lo_g), (0, 0))).astype(BF16),
            "r_k": row(r_k[l]), "gn_g": row(gn_g[l]), "gn_b": row(gn_b[l]),
        }
        last = l == depth - 1
        gf = row(norm_f)
        xp, _, shp, wkp = _layer(xp, zero_shift, zero_wkv, tp, wt, big, l, gf, last, False)
        xs, vrows, shs, wks = _layer(xs, state_tshift[l], state_wkv[l], ts, wt, big, l, gf, last, True)
        outs["tsh_p"].append(shp)
        outs["wkv_p"].append(wkp)
        outs["tsh_s"].append(shs)
        outs["wkv_s"].append(wks)
        outs["vrow_s"].append(vrows.reshape(bs, ts, d_a))
    return (xp.reshape(bp, tp, d), xs.reshape(bs, ts, d), jnp.stack(outs["tsh_p"]), jnp.stack(outs["wkv_p"]),
            jnp.stack(outs["tsh_s"]), jnp.stack(outs["wkv_s"]), jnp.stack(outs["vrow_s"]))
```

```python
import functools

import jax
import jax.numpy as jnp
from jax import lax
from jax.experimental import pallas as pl
from jax.experimental.pallas import tpu as pltpu

F32 = jnp.float32
BF16 = jnp.bfloat16

HEAD = 64
PAIR = 2 * HEAD
CHUNK = 64
WKV_STREAMS = 2
MLP_BLOCK = 128
PROJ_TN = 512
MXU_COLS = 256
RMS_EPS = 1e-5
LN_EPS = 1e-5
GN_EPS = 64e-5
DECAY_SCALE = 0.6065306597126334
VMEM_LIMIT = 56 * 1024 * 1024

NN = (((1,), (0,)), ((), ()))
NT = (((1,), (1,)), ((), ()))


def _params(sem):
    return pltpu.CompilerParams(dimension_semantics=sem, vmem_limit_bytes=VMEM_LIMIT)


def _dot(a, b, dims=NN):
    return lax.dot_general(a, b, dims, preferred_element_type=F32)


def _split(x):
    hi = x.astype(BF16)
    lo = (x - hi.astype(F32)).astype(BF16)
    return hi, lo


def _mm1(a, b, dims=NN):
    return _dot(a.astype(BF16), b.astype(BF16), dims)


def _mm2(a, b_exact):
    ah, al = _split(a)
    return _dot(ah, b_exact) + _dot(al, b_exact)


def _rms(x, g):
    return x * lax.rsqrt(jnp.mean(x * x, axis=-1, keepdims=True) + RMS_EPS) * g


def _gelu(x):
    return 0.5 * x * (1.0 + jnp.tanh(0.7978845608028654 * (x + 0.044715 * (x * x * x))))


def _sigmoid(x):
    return 1.0 / (1.0 + jnp.exp(-x))


def _identity(x):
    return x


def _row_tile(rows, cap):
    t = min(rows, cap)
    while rows % t:
        t //= 2
    return t


def _cast_cat_kernel(a_ref, b_ref, o_ref, *, n_a, n_pad):
    lane = 128
    tail = n_a // lane * lane
    o_ref[:, tail:n_a + n_pad] = jnp.zeros((o_ref.shape[0], n_a + n_pad - tail), o_ref.dtype)
    o_ref[:, :n_a] = a_ref[...].astype(o_ref.dtype)
    o_ref[:, n_a + n_pad:] = b_ref[...].astype(o_ref.dtype)


def cast_cat(a, b, n_pad):
    depth, kdim, n_a = a.shape
    n_b = b.shape[2]
    assert (n_a + n_pad) % 128 == 0
    tk = _row_tile(kdim, 128)
    return pl.pallas_call(
        functools.partial(_cast_cat_kernel, n_a=n_a, n_pad=n_pad),
        grid=(depth, kdim // tk),
        in_specs=[
            pl.BlockSpec((None, tk, n_a), lambda l, i: (l, i, 0)),
            pl.BlockSpec((None, tk, n_b), lambda l, i: (l, i, 0)),
        ],
        out_specs=pl.BlockSpec((None, tk, n_a + n_pad + n_b), lambda l, i: (l, i, 0)),
        out_shape=jax.ShapeDtypeStruct((depth, kdim, n_a + n_pad + n_b), BF16),
        compiler_params=_params(("parallel", "parallel")),
        name="cast_cat",
    )(a, b)


def _proj_kernel(x_ref, g_ref, w_ref, uv_ref, pj_ref, gt_ref, h_ref, *, n_uv, n_pj):
    j = pl.program_id(1)

    @pl.when(j == 0)
    def _():
        h_ref[...] = _rms(x_ref[...], g_ref[...]).astype(BF16)

    def emit(o_ref, act, row_parts):
        tr = o_ref.shape[0] // row_parts
        for r in range(0, o_ref.shape[0], tr):
            for c in range(0, o_ref.shape[1], MXU_COLS):
                y = _dot(h_ref[r:r + tr, :], w_ref[:, c:c + MXU_COLS])
                o_ref[r:r + tr, c:c + MXU_COLS] = act(y).astype(o_ref.dtype)

    @pl.when(j < n_uv)
    def _():
        emit(uv_ref, _gelu, 2)

    @pl.when((j >= n_uv) & (j < n_uv + n_pj))
    def _():
        emit(pj_ref, _identity, 1)

    @pl.when(j >= n_uv + n_pj)
    def _():
        emit(gt_ref, _identity, 1)


def proj(x, gain, w_cat, layer, n_uv_cols, n_pj_cols):
    rows, d = x.shape
    n = w_cat.shape[2]
    tn = PROJ_TN
    n_uv, n_pj = n_uv_cols // tn, n_pj_cols // tn
    n_gt = n // tn - n_uv - n_pj
    tm = _row_tile(rows, 1024)
    return pl.pallas_call(
        functools.partial(_proj_kernel, n_uv=n_uv, n_pj=n_pj),
        grid=(rows // tm, n // tn),
        in_specs=[
            pl.BlockSpec((tm, d), lambda i, j: (i, 0)),
            pl.BlockSpec((1, d), lambda i, j: (0, 0)),
            pl.BlockSpec((None, d, tn), lambda i, j: (layer, 0, j)),
        ],
        out_specs=[
            pl.BlockSpec((tm, tn), lambda i, j: (i, jnp.minimum(j, n_uv - 1))),
            pl.BlockSpec((tm, tn), lambda i, j: (i, jnp.clip(j - n_uv, 0, n_pj - 1))),
            pl.BlockSpec((tm, tn), lambda i, j: (i, jnp.clip(j - n_uv - n_pj, 0, n_gt - 1))),
        ],
        out_shape=[
            jax.ShapeDtypeStruct((rows, n_uv * tn), F32),
            jax.ShapeDtypeStruct((rows, n_pj * tn), F32),
            jax.ShapeDtypeStruct((rows, n_gt * tn), BF16),
        ],
        scratch_shapes=[pltpu.VMEM((tm, d), BF16)],
        compiler_params=_params(("parallel", "arbitrary")),
        name="proj",
    )(x, gain, w_cat)


def _gmlp_rows(u_ref, v_ref, ln_g, ln_b, ws_ref, bs_ref, blk):
    v = v_ref[...]
    mu = jnp.mean(v, axis=-1, keepdims=True)
    vc = v - mu
    var = jnp.mean(vc * vc, axis=-1, keepdims=True)
    vn = vc * lax.rsqrt(var + LN_EPS) * ln_g + ln_b
    vnb = vn.astype(BF16)
    q_idx = lax.broadcasted_iota(jnp.int32, (blk, blk), 0)
    k_idx = lax.broadcasted_iota(jnp.int32, (blk, blk), 1)
    causal = k_idx <= q_idx
    groups = ws_ref.shape[0]
    dg = v.shape[1] // groups
    cols = []
    for g in range(groups):
        wg = jnp.where(causal, ws_ref[g], 0.0).astype(BF16)
        bg = bs_ref[g]
        cs = slice(g * dg, (g + 1) * dg)
        parts = [(u_ref[c:c + blk, cs] * (_dot(wg, vnb[c:c + blk, cs]) + bg)).astype(BF16)
                 for c in range(0, v.shape[0], blk)]
        cols.append(jnp.concatenate(parts, axis=0))
    return jnp.concatenate(cols, axis=1), vn


def _rwkv_inputs(pb, prev_row, mu, w0, a0, k_k, k_a, wwa, g2, ones_bd, d_b):
    row = lax.broadcasted_iota(jnp.int32, pb.shape, 0)
    prev = jnp.where(row == 0, prev_row, pltpu.roll(pb, 1, 0))
    xs = pb + (prev - pb) * mu
    r = xs[:, 0:d_b]
    k = xs[:, d_b:2 * d_b]
    v = xs[:, 2 * d_b:3 * d_b]
    wa = xs[:, 3 * d_b:3 * d_b + PAIR]
    gl = xs[:, 3 * d_b + PAIR:]
    lane = lax.broadcasted_iota(jnp.int32, wa.shape, 1)
    lora = _mm1(jnp.where(lane < HEAD, jnp.tanh(wa), wa), wwa)
    lw = -DECAY_SCALE * _sigmoid(w0 + lora[:, :d_b])
    a = _sigmoid(a0 + lora[:, d_b:])
    g = _mm1(_sigmoid(gl), g2)
    kk = k * k_k
    sq = kk * kk
    ss = jnp.concatenate([_mm2(sq[:, c:c + PAIR], ones_bd) for c in range(0, d_b, PAIR)], axis=1)
    kk = kk * lax.rsqrt(jnp.maximum(ss, 1e-24))
    return r, lw, k * (1.0 + (a - 1.0) * k_a), v, kk, -(kk * a), g


def _wkv_chunk(r, lw, cum, k, v, p, q, g, rk, gng, gnb, S, consts):
    m0, bd_strict, sel_incl, bd_mask, eye, ones_bd, inv_masks = consts
    c2 = 2 * CHUNK
    pm = lambda f, *ls: [f(*a) for a in zip(*ls)]
    mm = _mm1
    zero = jnp.zeros((CHUNK, PAIR), F32)
    stack = lambda x: jnp.concatenate([jnp.where(m0, x, zero), jnp.where(m0, zero, x)], axis=0)
    pick = lambda blk: jnp.where(m0, blk[:CHUNK], blk[CHUNK:])

    cum_end = pm(lambda c: c[CHUNK - 1:CHUNK, :], cum)
    g_in = pm(jnp.exp, cum)
    g_ex = pm(lambda c, l: jnp.exp(c - l), cum, lw)
    g_inv = pm(lambda c: jnp.exp(-c), cum)
    g_bar = pm(lambda ce, c: jnp.exp(ce - c), cum_end, cum)
    mul = lambda a, b: a * b
    pt, rt = pm(mul, p, g_ex), pm(mul, r, g_in)
    qt, kt = pm(mul, q, g_inv), pm(mul, k, g_inv)
    qb, kb = pm(mul, q, g_bar), pm(mul, k, g_bar)

    a_all = pm(lambda a, b, c, d: mm(jnp.concatenate([stack(a), stack(b)], axis=0),
                                     jnp.concatenate([c, c, d, d], axis=0), NT), pt, rt, qt, kt)
    x_all = pm(lambda a, b, s: mm(jnp.concatenate([a, b], axis=0), s, NT), pt, rt, S)
    a_pq = pm(lambda a: jnp.where(bd_strict, a[:c2, :c2], 0.0), a_all)
    a_pk = pm(lambda a: jnp.where(bd_strict, a[:c2, c2:], 0.0), a_all)
    a_rq = pm(lambda a: jnp.where(sel_incl, pick(a[c2:, :c2]), 0.0), a_all)
    a_rk = pm(lambda a: jnp.where(sel_incl, pick(a[c2:, c2:]), 0.0), a_all)
    v_st = pm(stack, v)
    w0 = pm(mm, a_pk, v_st)

    t_acc = pm(lambda a: eye + jnp.where(inv_masks[0], a, 0.0), a_pq)
    for off in inv_masks[1:]:
        y = pm(lambda a, t: mm(jnp.where(off, a, 0.0), t), a_pq, t_acc)
        t_acc = pm(lambda t, yy: t + mm(t, yy), t_acc, y)

    u_st = pm(lambda t, x, w: mm(t, stack(x[:CHUNK]) + w), t_acc, x_all, w0)
    o = pm(lambda x, aq, ak, u, vs: x[CHUNK:] + mm(jnp.concatenate([aq, ak], axis=1),
                                                   jnp.concatenate([u, vs], axis=0)),
           x_all, a_rq, a_rk, u_st, v_st)
    upd = pm(lambda u, vv, a, b: mm(jnp.concatenate([u[:CHUNK] + u[CHUNK:], vv], axis=0).T,
                                    jnp.concatenate([a, b], axis=0)), u_st, v, qb, kb)
    s_new = pm(lambda s, ce, u: s * jnp.exp(ce) + jnp.where(bd_mask, u, 0.0), S, cum_end, upd)

    inv_n = 1.0 / HEAD
    mean = pm(lambda x: mm(x, ones_bd) * inv_n, o)
    oc = pm(lambda x, m: x - m, o, mean)
    var = pm(lambda x: mm(x * x, ones_bd) * inv_n, oc)
    bonus = pm(lambda a, b, c: mm(a * b * c, ones_bd), r, k, rk)
    y = pm(lambda x, s2, gg, gb, bo, vv, gt: (x * lax.rsqrt(s2 + GN_EPS) * gg + gb + bo * vv) * gt,
           oc, var, gng, gnb, bonus, v, g)
    return y, s_new


def _wkv_kernel(pj_ref, sh0_ref, mu_ref, w0_ref, a0_ref, kk_ref, ka_ref, wwa_ref, g2_ref, rk_ref, gng_ref, gnb_ref,
                s0_ref, y_ref, so_ref, s_scr, prev_scr, *, n_chunks, d_b):
    tb = pl.program_id(1)

    @pl.when(tb == 0)
    def _():
        s_scr[...] = s0_ref[...]
        prev_scr[...] = sh0_ref[...]

    lane = lax.broadcasted_iota(jnp.int32, (CHUNK, PAIR), 1)
    m0 = lane < HEAD
    c2 = 2 * CHUNK
    ri = lax.broadcasted_iota(jnp.int32, (c2, c2), 0)
    ci = lax.broadcasted_iota(jnp.int32, (c2, c2), 1)
    same = (ri // CHUNK) == (ci // CHUNK)
    bd_strict = same & ((ci % CHUNK) < (ri % CHUNK))
    bd_mask = (ri // HEAD) == (ci // HEAD)
    eye = jnp.where(ri == ci, 1.0, 0.0).astype(F32)
    ones_bd = jnp.where(bd_mask, 1.0, 0.0).astype(BF16)
    ti = lax.broadcasted_iota(jnp.int32, (CHUNK, c2), 0)
    si = lax.broadcasted_iota(jnp.int32, (CHUNK, c2), 1)
    sel_incl = (si % CHUNK) <= ti
    inv_masks = [(ri // 2) == (ci // 2)]
    blk = 2
    while blk < CHUNK:
        inv_masks.append(((ri // (2 * blk)) == (ci // (2 * blk))) & ((ri // blk) != (ci // blk)))
        blk *= 2
    consts = (m0, bd_strict, sel_incl, bd_mask, eye, ones_bd, inv_masks)
    tr = lax.broadcasted_iota(jnp.int32, (CHUNK, CHUNK), 0)
    tc = lax.broadcasted_iota(jnp.int32, (CHUNK, CHUNK), 1)
    tri = jnp.where(tc <= tr, 1.0, 0.0).astype(BF16)

    n_sb = pj_ref.shape[0]
    n_pairs = d_b // PAIR
    units = [(sb, pr, slice(pr * PAIR, (pr + 1) * PAIR)) for sb in range(n_sb) for pr in range(n_pairs)]

    def chunk_body(c, carry):
        rows = pl.ds(pl.multiple_of(c * CHUNK, CHUNK), CHUNK)
        per_sb = []
        for sb in range(n_sb):
            pb = pj_ref[sb, rows, :]
            r, lw, k, v, kk, q, g = _rwkv_inputs(
                pb, prev_scr[sb], mu_ref[...], w0_ref[...], a0_ref[...], kk_ref[...], ka_ref[...],
                wwa_ref[...], g2_ref[...], ones_bd, d_b)
            prev_scr[sb] = pb[CHUNK - 1:CHUNK, :]
            lw_hi, lw_lo = _split(lw)
            cum = _dot(tri, lw_hi) + _dot(tri, lw_lo)
            per_sb.append((r, lw, cum, k, v, kk, q, g))
        seq = lambda i: [per_sb[sb][i][:, ls] for sb, _, ls in units]
        vec = lambda ref: [ref[:, ls] for _, _, ls in units]
        y, s_new = _wkv_chunk(*[seq(i) for i in range(8)], vec(rk_ref), vec(gng_ref), vec(gnb_ref),
                              [s_scr[sb, pr] for sb, pr, _ in units], consts)
        for i, (sb, pr, ls) in enumerate(units):
            s_scr[sb, pr] = s_new[i]
            y_ref[sb, rows, ls] = y[i].astype(y_ref.dtype)
        return carry

    lax.fori_loop(0, n_chunks, chunk_body, 0)

    @pl.when(tb == pl.num_programs(1) - 1)
    def _():
        so_ref[...] = s_scr[...]


def wkv(pj, shift0, mu, w0, a0, k_k, k_a, wwa, g2p, rk, gng, gnb, s0_bd, t_len):
    rows, width = pj.shape
    d_b = w0.shape[1]
    n_streams = rows // t_len
    n_pairs = d_b // PAIR
    n_sb = WKV_STREAMS if n_streams % WKV_STREAMS == 0 else 1
    tb_len = _row_tile(t_len, 128)
    vec = lambda n: pl.BlockSpec((1, n), lambda s, t: (0, 0))
    full = lambda a: pl.BlockSpec(a.shape, lambda s, t: (0, 0))
    st = pl.BlockSpec((n_sb, n_pairs, PAIR, PAIR), lambda s, t: (s, 0, 0, 0))
    y, s_out = pl.pallas_call(
        functools.partial(_wkv_kernel, n_chunks=tb_len // CHUNK, d_b=d_b),
        grid=(n_streams // n_sb, t_len // tb_len),
        in_specs=[
            pl.BlockSpec((n_sb, tb_len, width), lambda s, t: (s, t, 0)),
            pl.BlockSpec((n_sb, 1, width), lambda s, t: (s, 0, 0)),
            vec(width), vec(d_b), vec(d_b), vec(d_b), vec(d_b),
            full(wwa), full(g2p), vec(d_b), vec(d_b), vec(d_b), st,
        ],
        out_specs=[pl.BlockSpec((n_sb, tb_len, d_b), lambda s, t: (s, t, 0)), st],
        out_shape=[
            jax.ShapeDtypeStruct((n_streams, t_len, d_b), BF16),
            jax.ShapeDtypeStruct(s0_bd.shape, F32),
        ],
        scratch_shapes=[pltpu.VMEM((n_sb, n_pairs, PAIR, PAIR), F32), pltpu.VMEM((n_sb, 1, width), F32)],
        compiler_params=_params(("parallel", "arbitrary")),
        name="wkv",
    )(pj.reshape(n_streams, t_len, width), shift0, mu, w0, a0, k_k, k_a, wwa, g2p, rk, gng, gnb, s0_bd)
    return y.reshape(rows, d_b), s_out


def _mix_out_kernel(x_ref, u_ref, v_ref, yb_ref, ga_ref, gb_ref, lg_ref, lb_ref, ws_ref, bs_ref, wa_ref, wb_ref,
                    wo_ref, g2_ref, o_ref, h_ref, *maybe_vn_ref, blk):
    ya, vn = _gmlp_rows(u_ref, v_ref, lg_ref[...], lb_ref[...], ws_ref, bs_ref, blk)
    for vn_ref in maybe_vn_ref:
        vn_ref[...] = vn
    gate = lambda ref: 0.5 + 0.5 * jnp.tanh(0.5 * ref[...].astype(F32))
    mixed = gate(ga_ref) * _dot(ya, wa_ref[...]) + gate(gb_ref) * _dot(yb_ref[...], wb_ref[...])
    x_new = x_ref[...] + _dot(mixed.astype(BF16), wo_ref[...])
    o_ref[...] = x_new
    h_ref[...] = _rms(x_new, g2_ref[...]).astype(h_ref.dtype)


def mix_out(x, uv, yb, gates, ln_g, ln_b, ws, bs_full, w_pa, w_pb, w_o, layer, norm2, blk, want_vn):
    rows, d = x.shape
    d_a, d_b = uv.shape[1] // 2, yb.shape[1]
    groups = ws.shape[0]
    tm = _row_tile(rows, max(blk, 256))
    n_out = 3 if want_vn else 2
    resident = lambda a: pl.BlockSpec((None,) + a.shape[1:], lambda i: (layer, 0, 0), pipeline_mode=pl.Buffered(1))
    return pl.pallas_call(
        functools.partial(_mix_out_kernel, blk=blk),
        grid=(rows // tm,),
        in_specs=[
            pl.BlockSpec((tm, d), lambda i: (i, 0)),
            pl.BlockSpec((tm, d_a), lambda i: (i, 0)),
            pl.BlockSpec((tm, d_a), lambda i: (i, 1)),
            pl.BlockSpec((tm, d_b), lambda i: (i, 0)),
            pl.BlockSpec((tm, d), lambda i: (i, 0)),
            pl.BlockSpec((tm, d), lambda i: (i, 1)),
            pl.BlockSpec((1, d_a), lambda i: (0, 0)),
            pl.BlockSpec((1, d_a), lambda i: (0, 0)),
            pl.BlockSpec((groups, blk, blk), lambda i: (0, 0, 0)),
            pl.BlockSpec((groups, blk, d_a // groups), lambda i: (0, 0, 0)),
            resident(w_pa), resident(w_pb), resident(w_o),
            pl.BlockSpec((1, d), lambda i: (0, 0)),
        ],
        out_specs=[pl.BlockSpec((tm, d), lambda i: (i, 0)), pl.BlockSpec((tm, d), lambda i: (i, 0)),
                   pl.BlockSpec((tm, d_a), lambda i: (i, 0))][:n_out],
        out_shape=[jax.ShapeDtypeStruct((rows, d), F32), jax.ShapeDtypeStruct((rows, d), BF16),
                   jax.ShapeDtypeStruct((rows, d_a), F32)][:n_out],
        compiler_params=_params(("parallel",)),
        name="mix_out",
    )(x, uv, uv, yb, gates, gates, ln_g, ln_b, ws, bs_full, w_pa, w_pb, w_o, norm2)


def _ffn_kernel(x_ref, h_ref, wu_ref, wd_ref, gf_ref, o_ref, acc_ref, *, final_norm):
    f = pl.program_id(1)

    @pl.when(f == 0)
    def _():
        acc_ref[...] = jnp.zeros_like(acc_ref)

    up = jnp.maximum(_dot(h_ref[...], wu_ref[...]), 0.0)
    acc_ref[...] += _dot((up * up).astype(BF16), wd_ref[...])

    @pl.when(f == pl.num_programs(1) - 1)
    def _():
        y = x_ref[...] + acc_ref[...]
        if final_norm:
            y = _rms(y, gf_ref[...])
        o_ref[...] = y


def ffn(x, h, w_up, w_down, layer, final_gain, final_norm, tf=1024):
    rows, d = x.shape
    d_ff = w_up.shape[2]
    tm = _row_tile(rows, 512)
    return pl.pallas_call(
        functools.partial(_ffn_kernel, final_norm=final_norm),
        grid=(rows // tm, d_ff // tf),
        in_specs=[
            pl.BlockSpec((tm, d), lambda i, f: (i, 0)),
            pl.BlockSpec((tm, d), lambda i, f: (i, 0)),
            pl.BlockSpec((None, d, tf), lambda i, f: (layer, 0, f)),
            pl.BlockSpec((None, tf, d), lambda i, f: (layer, f, 0)),
            pl.BlockSpec((1, d), lambda i, f: (0, 0)),
        ],
        out_specs=pl.BlockSpec((tm, d), lambda i, f: (i, 0)),
        out_shape=jax.ShapeDtypeStruct((rows, d), F32),
        scratch_shapes=[pltpu.VMEM((tm, d), F32)],
        compiler_params=_params(("parallel", "arbitrary")),
        name="ffn",
    )(x, h, w_up, w_down, final_gain)


def _to_block_diag(s):
    b, h, n, _ = s.shape
    sr = s.reshape(b, h // 2, 2, n, n)
    z = jnp.zeros_like(sr[:, :, 0])
    top = jnp.concatenate([sr[:, :, 0], z], axis=-1)
    bot = jnp.concatenate([z, sr[:, :, 1]], axis=-1)
    return jnp.concatenate([top, bot], axis=-2)


def _from_block_diag(sbd):
    b, hp, n2, _ = sbd.shape
    n = n2 // 2
    return jnp.stack([sbd[:, :, :n, :n], sbd[:, :, n:, n:]], axis=2).reshape(b, 2 * hp, n, n)


def _layer(x, shift0, s0, t_len, wt, big, layer, final_gain, final_norm, want_vn):
    rows = x.shape[0]
    n_streams = rows // t_len
    d_a = wt["ln_g"].shape[1]
    d_shift = shift0.shape[1]
    width = wt["mu"].shape[1]

    uv, pj, gates = proj(x, wt["norm1"], big["w_cat"], layer, 2 * d_a, width)

    shift0p = jnp.pad(shift0, ((0, 0), (0, width - d_shift)))[:, None, :]
    yb, s_bd = wkv(pj, shift0p, wt["mu"], wt["w0"], wt["a0"], wt["k_k"], wt["k_a"], wt["wwa"], wt["g2p"],
                   wt["r_k"], wt["gn_g"], wt["gn_b"], _to_block_diag(s0), t_len)

    blk = MLP_BLOCK if t_len % MLP_BLOCK == 0 else t_len
    x, h2, *vn = mix_out(x, uv, yb, gates, wt["ln_g"], wt["ln_b"], wt["w_s"][:, :blk, :blk], wt["b_full"][:, :blk],
                         big["w_pa"], big["w_pb"], big["w_o"], layer, wt["norm2"], blk, want_vn)
    x = ffn(x, h2, big["w_up"], big["w_down"], layer, final_gain, final_norm)

    new_shift = pj.reshape(n_streams, t_len, width)[:, -1, :d_shift]
    return x, (vn[0] if want_vn else None), new_shift, _from_block_diag(s_bd)


def kernel(x_prompt, x_sample, state_tshift, state_wkv, norm1, w_in, ln_v_g, ln_v_b, w_s, b_s, mu_shift,
           w0, w2, a0, a2, g2, k_k, k_a, r_k, gn_g, gn_b, w_gate, w_pa, w_pb, w_o, norm2, w_up, w_down,
           norm_f):
    depth = w_in.shape[0]
    bp, tp, d = x_prompt.shape
    bs, ts, _ = x_sample.shape
    d_a = ln_v_g.shape[1]
    d_b = w0.shape[1]
    d_shift = mu_shift.shape[1]
    heads = d_b // HEAD
    assert tp % CHUNK == 0 and ts % CHUNK == 0 and heads % 2 == 0
    assert all(t % MLP_BLOCK == 0 or (t < MLP_BLOCK and t % 8 == 0) for t in (tp, ts))
    lo_w, lo_a = w2.shape[1], a2.shape[1]
    assert lo_w == HEAD and lo_a == HEAD
    lo_g = g2.shape[1]
    width = -(-d_shift // PROJ_TN) * PROJ_TN
    g_w = width - (3 * d_b + PAIR)
    dg = d_a // w_s.shape[1]

    row = lambda a: a.reshape(1, -1)

    xp = x_prompt.reshape(bp * tp, d)
    xs = x_sample.reshape(bs * ts, d)
    zero_shift = jnp.zeros((bp, d_shift), F32)
    zero_wkv = jnp.zeros((bp, heads, HEAD, HEAD), F32)
    outs = {n: [] for n in ("tsh_p", "wkv_p", "tsh_s", "wkv_s", "vrow_s")}
    big = {
        "w_cat": cast_cat(w_in, w_gate, width - d_shift),
        "w_pa": w_pa.astype(BF16), "w_pb": w_pb.astype(BF16), "w_o": w_o.astype(BF16),
        "w_up": w_up.astype(BF16), "w_down": w_down.astype(BF16),
    }
    for l in range(depth):
        wwa = jnp.zeros((PAIR, 2 * d_b), F32)
        wwa = wwa.at[:lo_w, :d_b].set(w2[l]).at[lo_w:, d_b:].set(a2[l])
        wt = {
            "norm1": row(norm1[l]), "norm2": row(norm2[l]),
            "ln_g": row(ln_v_g[l]), "ln_b": row(ln_v_b[l]),
            "w_s": w_s[l],
            "b_full": jnp.broadcast_to(b_s[l][:, :, None], b_s[l].shape + (dg,)),
            "mu": row(jnp.pad(mu_shift[l], (0, width - d_shift))),
            "w0": row(w0[l]), "a0": row(a0[l]), "k_k": row(k_k[l]), "k_a": row(k_a[l]),
            "wwa": wwa.astype(BF16),
            "g2p": jnp.pad(g2[l], ((0, g_w - lo_g), (0, 0))).astype(BF16),
            "r_k": row(r_k[l]), "gn_g": row(gn_g[l]), "gn_b": row(gn_b[l]),
        }
        last = l == depth - 1
        gf = row(norm_f)
        xp, _, shp, wkp = _layer(xp, zero_shift, zero_wkv, tp, wt, big, l, gf, last, False)
        xs, vrows, shs, wks = _layer(xs, state_tshift[l], state_wkv[l], ts, wt, big, l, gf, last, True)
        outs["tsh_p"].append(shp)
        outs["wkv_p"].append(wkp)
        outs["tsh_s"].append(shs)
        outs["wkv_s"].append(wks)
        outs["vrow_s"].append(vrows.reshape(bs, ts, d_a))
    return (xp.reshape(bp, tp, d), xs.reshape(bs, ts, d), jnp.stack(outs["tsh_p"]), jnp.stack(outs["wkv_p"]),
            jnp.stack(outs["tsh_s"]), jnp.stack(outs["wkv_s"]), jnp.stack(outs["vrow_s"]))
```
